```python
import jax, jax.numpy as jnp
from jax import lax
import numpy as np


D_MODEL = 4096
BATCH = 4
SEQ = 4096
DEPTH = 4

HEAD_DIM = 128
D_MIX = D_MODEL
N_MIXERS = 4
GROUP_HEADS = D_MIX // (N_MIXERS * HEAD_DIM)
GROUP_WIDTH = GROUP_HEADS * HEAD_DIM
RET_HEADS = GROUP_HEADS
RET_CHUNK = 128
RET_ROT_BASE = 10000.0
SB_HEADS = GROUP_HEADS
SB_Q_BLOCK = 128
LRU_WIDTH = GROUP_WIDTH
LRU_BLOCKS = GROUP_HEADS
LRU_BLOCK_W = LRU_WIDTH // LRU_BLOCKS
LRU_CONV = 4
LRU_C = 8.0
NSA_HEADS = GROUP_HEADS
NSA_KV_HEADS = 2
NSA_KV_WIDTH = NSA_KV_HEADS * HEAD_DIM
CMP_LEN = 32
CMP_STRIDE = 16
CMP_HIDDEN = 256
SLC_LEN = 64
SLC_TOPK = 16
SLC_Q_BLOCK = 64
WINDOW = 512
WIN_Q_BLOCK = 128
ROPE_THETA = 500000.0
ROPE_DIMS = HEAD_DIM // 4
D_FF = 256 * ((8 * D_MODEL // 3 + 255) // 256)
MLP_CONV = 3
NORM_EPS = 1e-6
SPLIT_SIZES = tuple(4 * [GROUP_WIDTH] + 3 * [GROUP_WIDTH] + 2 * [LRU_WIDTH] + [GROUP_WIDTH] + 6 * [NSA_KV_WIDTH] + [3 * NSA_HEADS])
N_IN = sum(SPLIT_SIZES)

kernel_name = 'hybrid_parallel_heads_decoder'

F32 = jnp.float32


def rms_norm(x, w):
    xf = x.astype(F32)
    y = xf * lax.rsqrt(jnp.mean(xf * xf, axis=-1, keepdims=True) + NORM_EPS)
    return (y * w.astype(F32)).astype(x.dtype)


def rms_unit(x):
    xf = x.astype(F32)
    return xf * lax.rsqrt(jnp.mean(xf * xf, axis=-1, keepdims=True) + NORM_EPS)


def split_heads(x, n_heads):
    b, s, _ = x.shape
    return x.reshape(b, s, n_heads, -1).transpose(0, 2, 1, 3)


def merge_heads(x):
    b, h, s, d = x.shape
    return x.transpose(0, 2, 1, 3).reshape(b, s, h * d)


def apply_rotary(x, pos, inv_freq):
    half = inv_freq.shape[0]
    ang = pos.astype(F32)[:, None] * inv_freq[None, :]
    cos, sin = jnp.cos(ang), jnp.sin(ang)
    xf = x.astype(F32)
    x1, x2 = xf[..., :half], xf[..., half:2 * half]
    out = jnp.concatenate([x1 * cos - x2 * sin, x2 * cos + x1 * sin, xf[..., 2 * half:]], axis=-1)
    return out.astype(x.dtype)


def partial_rope_freqs():
    half = ROPE_DIMS // 2
    return ROPE_THETA ** (-jnp.arange(half, dtype=F32) / half)


def masked_softmax(s, mask):
    s = jnp.where(mask, s.astype(F32), -jnp.inf)
    m = jnp.max(s, axis=-1, keepdims=True)
    m = jnp.where(jnp.isfinite(m), m, 0.0)
    e = jnp.where(mask, jnp.exp(s - m), 0.0)
    return e / jnp.maximum(jnp.sum(e, axis=-1, keepdims=True), 1e-30)


def causal_dwconv(x, w, b):
    width, s = w.shape[0], x.shape[1]
    xp = jnp.pad(x, ((0, 0), (width - 1, 0), (0, 0)))
    y = b.astype(x.dtype)
    for k in range(width):
        y = y + xp[:, k:k + s] * w[k].astype(x.dtype)
    return y


def retention_chunkwise(q, k, v):
    b, h, s, d = q.shape
    c = RET_CHUNK
    n = s // c
    log_g = jnp.log(1.0 - 2.0 ** (-5.0 - jnp.arange(h, dtype=F32)))
    idx = jnp.arange(c, dtype=F32)
    rel = idx[:, None] - idx[None, :]
    inner_decay = jnp.where(rel >= 0, jnp.exp(log_g[:, None, None] * jnp.maximum(rel, 0.0)), 0.0)
    q_decay = jnp.exp(log_g[:, None] * (idx + 1.0))[..., None]
    k_decay = jnp.exp(log_g[:, None] * (c - 1.0 - idx))[..., None]
    chunk_decay = jnp.exp(log_g * c)[:, None, None]
    to_chunks = lambda t: t.reshape(b, h, n, c, d).transpose(2, 0, 1, 3, 4)

    def step(state, inp):
        qb, kb, vb = inp
        inner = jnp.einsum('bhid,bhjd->bhij', qb, kb) * inner_decay
        o = jnp.einsum('bhij,bhjd->bhid', inner, vb) + jnp.einsum('bhid,bhde->bhie', qb, state) * q_decay
        state = state * chunk_decay + jnp.einsum('bhjd,bhje->bhde', kb * k_decay, vb)
        return state, o

    state0 = jnp.zeros((b, h, d, d), F32)
    _, o = lax.scan(step, state0, (to_chunks(q), to_chunks(k), to_chunks(v)))
    return o.transpose(1, 2, 0, 3, 4).reshape(b, h, s, d)


def retention_mixer(q, k, v, g):
    s = q.shape[1]
    pos = jnp.arange(s)
    inv = 1.0 / (RET_ROT_BASE ** jnp.linspace(0.0, 1.0, HEAD_DIM // 2, dtype=F32))
    qh = apply_rotary(split_heads(q, RET_HEADS), pos, inv).astype(F32)
    kh = apply_rotary(split_heads(k, RET_HEADS), pos, inv).astype(F32) * (HEAD_DIM ** -0.5)
    vh = split_heads(v, RET_HEADS).astype(F32)
    o = rms_unit(retention_chunkwise(qh, kh, vh))
    return merge_heads(o) * jax.nn.silu(g.astype(F32))


def stick_breaking_mixer(q, k, v):
    qh = split_heads(q, SB_HEADS).astype(F32)
    kh = split_heads(k, SB_HEADS).astype(F32)
    vh = split_heads(v, SB_HEADS).astype(F32)
    b, h, s, d = qh.shape
    nb = s // SB_Q_BLOCK
    scale = d ** -0.5
    key_pos = jnp.arange(s)
    q_blocks = qh.reshape(b, h, nb, SB_Q_BLOCK, d).transpose(2, 0, 1, 3, 4)

    def block(args):
        i, qi = args
        qpos = i * SB_Q_BLOCK + jnp.arange(SB_Q_BLOCK)
        z = jnp.einsum('bhqd,bhkd->bhqk', qi, kh) * scale
        past = key_pos[None, :] < qpos[:, None]
        log_beta = jax.nn.log_sigmoid(z)
        log_stay_j = jnp.where(past, jax.nn.log_sigmoid(-z), 0.0)
        log_stay = lax.cumsum(log_stay_j, axis=3, reverse=True) - log_stay_j
        w = jnp.where(past, jnp.exp(log_beta + log_stay), 0.0)
        return jnp.einsum('bhqk,bhkd->bhqd', w, vh)

    o = lax.map(block, (jnp.arange(nb), q_blocks))
    o = o.transpose(1, 2, 0, 3, 4).reshape(b, h, s, d)
    return merge_heads(o)


def rg_lru(x, w_a, b_a, w_x, b_x, lam):
    b, s, c = x.shape
    xb = x.reshape(b, s, LRU_BLOCKS, LRU_BLOCK_W)
    r = jax.nn.sigmoid((jnp.einsum('bsnc,ncd->bsnd', xb, w_a).reshape(b, s, c) + b_a).astype(F32))
    i = jax.nn.sigmoid((jnp.einsum('bsnc,ncd->bsnd', xb, w_x).reshape(b, s, c) + b_x).astype(F32))
    log_a = LRU_C * r * jax.nn.log_sigmoid(lam.astype(F32))
    a = jnp.exp(log_a)
    pos = jnp.arange(s)[:, None]
    mult = jnp.where(pos == 0, 1.0, jnp.sqrt(jnp.maximum(-jnp.expm1(2.0 * log_a), 0.0)))
    u = mult * (i * x.astype(F32))

    def combine(left, right):
        a1, b1 = left
        a2, b2 = right
        return a1 * a2, a2 * b1 + b2

    _, hseq = lax.associative_scan(combine, (a, u), axis=1)
    return hseq


def rglru_mixer(gate_in, rec_in, conv_w, conv_b, w_a, b_a, w_x, b_x, lam):
    xr = causal_dwconv(rec_in, conv_w, conv_b)
    hseq = rg_lru(xr, w_a, b_a, w_x, b_x, lam)
    return hseq * jax.nn.gelu(gate_in.astype(F32))


def nsa_compress(x, pos_emb, w1, w2):
    s = x.shape[2]
    nc = (s - CMP_LEN) // CMP_STRIDE + 1
    idx = np.arange(nc)[:, None] * CMP_STRIDE + np.arange(CMP_LEN)[None, :]
    blocks = x[:, :, idx] + pos_emb
    hid = jax.nn.gelu(jnp.einsum('bgnld,ldh->bgnh', blocks, w1))
    return jnp.einsum('bgnh,hd->bgnd', hid, w2)


def nsa_mixer(q, kc, vc, ks, vs, kw, vw, gate_logits, q_norm_w, k_norm_w, cmp_pos, cmp_w1, cmp_w2):
    b, s, _ = q.shape
    h, g, d = NSA_HEADS, NSA_KV_HEADS, HEAD_DIM
    r = h // g
    scale = d ** -0.5
    pos = jnp.arange(s)
    inv = partial_rope_freqs()
    qg = apply_rotary(rms_norm(split_heads(q, h), q_norm_w), pos, inv).reshape(b, g, r, s, d)

    nc = (s - CMP_LEN) // CMP_STRIDE + 1
    cmp_end = jnp.arange(nc) * CMP_STRIDE + CMP_LEN - 1
    k_cmp = nsa_compress(split_heads(kc, g), cmp_pos[0], cmp_w1[0], cmp_w2[0])
    v_cmp = nsa_compress(split_heads(vc, g), cmp_pos[1], cmp_w1[1], cmp_w2[1])
    k_cmp = apply_rotary(rms_norm(k_cmp, k_norm_w[0]), cmp_end, inv)
    s_cmp = jnp.einsum('bgrtd,bgnd->bgrtn', qg, k_cmp) * scale
    p_cmp = masked_softmax(s_cmp, cmp_end[None, :] <= pos[:, None])
    o_cmp = jnp.einsum('bgrtn,bgnd->bgrtd', p_cmp, v_cmp)

    n_slc = s // SLC_LEN
    ci = np.arange(nc)[:, None] * CMP_STRIDE
    sj = np.arange(n_slc)[None, :] * SLC_LEN
    overlap = np.maximum(0, np.minimum(ci + CMP_LEN, sj + SLC_LEN) - np.maximum(ci, sj)) / CMP_STRIDE
    p_slc = jnp.einsum('bgrtn,nj->bgtj', p_cmp, jnp.asarray(overlap, F32))
    blk = jnp.arange(n_slc)[None, :]
    cur = (pos // SLC_LEN)[:, None]
    valid_blk = blk <= cur
    forced = (blk == 0) | (blk == cur) | (blk == cur - 1)
    sel_score = jnp.where(forced, jnp.inf, jnp.where(valid_blk, p_slc, -jnp.inf))
    k_sel = min(SLC_TOPK, n_slc)
    _, sel_idx = lax.top_k(sel_score, k_sel)

    ksb = apply_rotary(rms_norm(split_heads(ks, g), k_norm_w[1]), pos, inv).reshape(b, g, n_slc, SLC_LEN, d)
    vsb = split_heads(vs, g).reshape(b, g, n_slc, SLC_LEN, d)
    nq = s // SLC_Q_BLOCK
    q_blocks = qg.reshape(b, g, r, nq, SLC_Q_BLOCK, d).transpose(3, 0, 1, 2, 4, 5)
    idx_blocks = sel_idx.reshape(b, g, nq, SLC_Q_BLOCK, k_sel).transpose(2, 0, 1, 3, 4)
    b_ix = jnp.arange(b)[:, None, None, None]
    g_ix = jnp.arange(g)[None, :, None, None]

    def slc_block(args):
        c, qi, ids = args
        tq = c * SLC_Q_BLOCK + jnp.arange(SLC_Q_BLOCK)
        kb = ksb[b_ix, g_ix, ids].reshape(b, g, SLC_Q_BLOCK, k_sel * SLC_LEN, d)
        vb = vsb[b_ix, g_ix, ids].reshape(b, g, SLC_Q_BLOCK, k_sel * SLC_LEN, d)
        tok = ids[..., None] * SLC_LEN + jnp.arange(SLC_LEN)
        mask = (tok <= tq[:, None, None]).reshape(b, g, 1, SLC_Q_BLOCK, k_sel * SLC_LEN)
        sc = jnp.einsum('bgrqd,bgqnd->bgrqn', qi, kb) * scale
        p = masked_softmax(sc, mask)
        return jnp.einsum('bgrqn,bgqnd->bgrqd', p, vb)

    o_slc = lax.map(slc_block, (jnp.arange(nq), q_blocks, idx_blocks))
    o_slc = o_slc.transpose(1, 2, 3, 0, 4, 5).reshape(b, g, r, s, d)

    kwh = apply_rotary(rms_norm(split_heads(kw, g), k_norm_w[2]), pos, inv)
    vwh = split_heads(vw, g)
    pad = ((0, 0), (0, 0), (WINDOW, 0), (0, 0))
    kwp, vwp = jnp.pad(kwh, pad), jnp.pad(vwh, pad)
    nw = s // WIN_Q_BLOCK
    span = WINDOW + WIN_Q_BLOCK
    qw_blocks = qg.reshape(b, g, r, nw, WIN_Q_BLOCK, d).transpose(3, 0, 1, 2, 4, 5)

    def win_block(args):
        c, qi = args
        start = c * WIN_Q_BLOCK
        kb = lax.dynamic_slice_in_dim(kwp, start, span, axis=2)
        vb = lax.dynamic_slice_in_dim(vwp, start, span, axis=2)
        tq = start + jnp.arange(WIN_Q_BLOCK)
        kpos = start - WINDOW + jnp.arange(span)
        dist = tq[:, None] - kpos[None, :]
        mask = (dist >= 0) & (dist < WINDOW) & (kpos[None, :] >= 0)
        sc = jnp.einsum('bgrqd,bgkd->bgrqk', qi, kb) * scale
        p = masked_softmax(sc, mask)
        return jnp.einsum('bgrqk,bgkd->bgrqd', p, vb)

    o_win = lax.map(win_block, (jnp.arange(nw), qw_blocks))
    o_win = o_win.transpose(1, 2, 3, 0, 4, 5).reshape(b, g, r, s, d)

    gates = jax.nn.sigmoid(gate_logits.astype(F32)).reshape(b, s, g, r, 3).transpose(0, 2, 3, 1, 4)
    o = gates[..., 0:1] * o_cmp + gates[..., 1:2] * o_slc + gates[..., 2:3] * o_win
    return merge_heads(o.reshape(b, h, s, d))


def setup_inputs(seed: int = 0) -> dict:
    key = jax.random.key(seed)
    ks = jax.random.split(key, 24)
    L = DEPTH

    def nrm(k, shape, fan_in):
        return jax.random.normal(k, shape, F32) * (fan_in ** -0.5)

    def gain(k, shape):
        return 1.0 + 0.02 * jax.random.normal(k, shape, F32)

    def bias(k, shape):
        return 0.02 * jax.random.normal(k, shape, F32)

    a_c = jax.random.uniform(ks[9], (L, LRU_WIDTH), F32, 0.9, 0.999)
    a = a_c ** (1.0 / LRU_C)
    return {
        'x': jax.random.normal(ks[0], (BATCH, SEQ, D_MODEL), F32),
        'attn_norm_w': gain(ks[1], (L, D_MODEL)),
        'w_in': nrm(ks[2], (L, D_MODEL, N_IN), D_MODEL),
        'lru_conv_w': nrm(ks[3], (L, LRU_CONV, LRU_WIDTH), LRU_CONV),
        'lru_conv_b': bias(ks[4], (L, LRU_WIDTH)),
        'lru_w_a': nrm(ks[5], (L, LRU_BLOCKS, LRU_BLOCK_W, LRU_BLOCK_W), LRU_BLOCK_W),
        'lru_b_a': bias(ks[6], (L, LRU_WIDTH)),
        'lru_w_x': nrm(ks[7], (L, LRU_BLOCKS, LRU_BLOCK_W, LRU_BLOCK_W), LRU_BLOCK_W),
        'lru_b_x': bias(ks[8], (L, LRU_WIDTH)),
        'lru_lambda': jnp.log(a) - jnp.log1p(-a),
        'nsa_q_norm_w': gain(ks[10], (L, HEAD_DIM)),
        'nsa_k_norm_w': gain(ks[11], (L, 3, HEAD_DIM)),
        'nsa_cmp_pos': bias(ks[12], (L, 2, CMP_LEN, HEAD_DIM)),
        'nsa_cmp_w1': nrm(ks[13], (L, 2, CMP_LEN, HEAD_DIM, CMP_HIDDEN), CMP_LEN * HEAD_DIM),
        'nsa_cmp_w2': nrm(ks[14], (L, 2, CMP_HIDDEN, HEAD_DIM), CMP_HIDDEN),
        'w_out': nrm(ks[15], (L, D_MIX, D_MODEL), D_MIX),
        'mlp_norm_w': gain(ks[16], (L, D_MODEL)),
        'w_gate': nrm(ks[17], (L, D_MODEL, D_FF), D_MODEL),
        'w_up': nrm(ks[18], (L, D_MODEL, D_FF), D_MODEL),
        'mlp_conv_w': nrm(ks[19], (L, MLP_CONV, D_FF), MLP_CONV),
        'mlp_conv_b': bias(ks[20], (L, D_FF)),
        'w_down': nrm(ks[21], (L, D_FF, D_MODEL), D_FF),
    }


def reference(x, attn_norm_w, w_in, lru_conv_w, lru_conv_b, lru_w_a, lru_b_a, lru_w_x, lru_b_x, lru_lambda,
              nsa_q_norm_w, nsa_k_norm_w, nsa_cmp_pos, nsa_cmp_w1, nsa_cmp_w2, w_out, mlp_norm_w,
              w_gate, w_up, mlp_conv_w, mlp_conv_b, w_down):
    offsets = np.cumsum(SPLIT_SIZES)[:-1].tolist()
    for l in range(DEPTH):
        hn = rms_norm(x, attn_norm_w[l])
        proj = jnp.einsum('bsd,de->bse', hn, w_in[l])
        (rq, rk, rv, rg, sq, sk, sv, lg, lr, nq, nkc, nvc, nks, nvs, nkw, nvw, ngate) = jnp.split(proj, offsets, axis=-1)
        y_ret = retention_mixer(rq, rk, rv, rg)
        y_sb = stick_breaking_mixer(sq, sk, sv)
        y_lru = rglru_mixer(lg, lr, lru_conv_w[l], lru_conv_b[l], lru_w_a[l], lru_b_a[l],
                            lru_w_x[l], lru_b_x[l], lru_lambda[l])
        y_nsa = nsa_mixer(nq, nkc, nvc, nks, nvs, nkw, nvw, ngate, nsa_q_norm_w[l], nsa_k_norm_w[l],
                          nsa_cmp_pos[l], nsa_cmp_w1[l], nsa_cmp_w2[l])
        mix = jnp.concatenate([y_ret, y_sb, y_lru, y_nsa], axis=-1).astype(x.dtype)
        x = x + jnp.einsum('bse,ed->bsd', mix, w_out[l]).astype(x.dtype)
        hn = rms_norm(x, mlp_norm_w[l])
        gt = causal_dwconv(jnp.einsum('bsd,df->bsf', hn, w_gate[l]), mlp_conv_w[l], mlp_conv_b[l])
        up = jnp.einsum('bsd,df->bsf', hn, w_up[l])
        x = x + jnp.einsum('bsf,fd->bsd', jax.nn.silu(gt) * up, w_down[l]).astype(x.dtype)
    return x
```

```python
import functools
import math

import numpy as np
import jax
import jax.numpy as jnp
from jax import lax
from jax.experimental import pallas as pl
from jax.experimental.pallas import tpu as pltpu

F32 = jnp.float32
BF16 = jnp.bfloat16

HEAD_DIM = 128
N_MIXERS = 4
RET_CHUNK = 128
RET_ROT_BASE = 10000.0
LRU_CONV = 4
LRU_C = 8.0
NSA_KV_HEADS = 2
CMP_LEN = 32
CMP_STRIDE = 16
SLC_LEN = 64
SLC_TOPK = 16
WINDOW = 512
ROPE_THETA = 500000.0
ROPE_DIMS = HEAD_DIM // 4
MLP_CONV = 3
NORM_EPS = 1e-6

LANES = 128
SUBLANES = 8
VMEM_LIMIT_BYTES = 56 * 1024 * 1024
NEG_BIG = -1e30
FORCED_SCORE = 1e30


def _params(*sem):
    return pltpu.CompilerParams(dimension_semantics=sem, vmem_limit_bytes=VMEM_LIMIT_BYTES)


def _round_up(x, m):
    return (x + m - 1) // m * m


def _dot(a, b):
    return jnp.dot(a, b, preferred_element_type=F32)


def _dot_nt(a, b):
    return lax.dot_general(a, b, (((1,), (1,)), ((), ())), preferred_element_type=F32)


def _dot_split(x, m):
    hi = x.astype(BF16)
    lo = (x - hi.astype(F32)).astype(BF16)
    return _dot(hi, m) + _dot(lo, m)


def _sigmoid(x):
    return 1.0 / (1.0 + jnp.exp(-x))


def _softplus(x):
    return jnp.maximum(x, 0.0) + jnp.log1p(jnp.exp(-jnp.abs(x)))


def _gelu_tanh(x):
    return x * (0.5 * (1.0 + jnp.tanh(math.sqrt(2.0 / math.pi) * (x + 0.044715 * (x * x * x)))))


def _rms_rows(x):
    return x * lax.rsqrt(jnp.mean(x * x, axis=-1, keepdims=True) + NORM_EPS)


def _rmsnorm_body(x_ref, w_ref, o_ref):
    o_ref[...] = (_rms_rows(x_ref[...]) * w_ref[0]).astype(o_ref.dtype)


def _rmsnorm(x, w3, layer):
    t, d = x.shape
    tr = 256
    return pl.pallas_call(
        _rmsnorm_body,
        grid=(t // tr,),
        in_specs=[pl.BlockSpec((tr, d), lambda i: (i, 0)),
                  pl.BlockSpec((1, 1, d), lambda i: (layer, 0, 0))],
        out_specs=pl.BlockSpec((tr, d), lambda i: (i, 0)),
        out_shape=jax.ShapeDtypeStruct((t, d), BF16),
        compiler_params=_params("parallel"),
        name="rmsnorm",
    )(x, w3)


def _matmul_body(a_ref, w_ref, o_ref):
    o_ref[...] = _dot(a_ref[...], w_ref[0]).astype(o_ref.dtype)


def _matmul(a, w, layer, tm, tn, out_dtype, name):
    t, k = a.shape
    n = w.shape[2]
    return pl.pallas_call(
        _matmul_body,
        grid=(n // tn, t // tm),
        in_specs=[pl.BlockSpec((tm, k), lambda j, i: (i, 0)),
                  pl.BlockSpec((1, k, tn), lambda j, i: (layer, 0, j))],
        out_specs=pl.BlockSpec((tm, tn), lambda j, i: (i, j)),
        out_shape=jax.ShapeDtypeStruct((t, n), out_dtype),
        compiler_params=_params("parallel", "parallel"),
        name=name,
    )(a, w)


def _outproj_body(y0_ref, y1_ref, y2_ref, y3_ref, w_ref, x_ref, o_ref):
    gw = y0_ref.shape[1]
    acc = x_ref[...]
    for n, y_ref in enumerate((y0_ref, y1_ref, y2_ref, y3_ref)):
        acc = acc + _dot(y_ref[...], w_ref[0, n * gw:(n + 1) * gw, :])
    o_ref[...] = acc


def _outproj(ys, w, x, layer, tm, tn):
    t, d = x.shape
    gw = ys[0].shape[1]
    y_spec = pl.BlockSpec((tm, gw), lambda j, i: (i, 0))
    return pl.pallas_call(
        _outproj_body,
        grid=(d // tn, t // tm),
        in_specs=[y_spec, y_spec, y_spec, y_spec,
                  pl.BlockSpec((1, N_MIXERS * gw, tn), lambda j, i: (layer, 0, j)),
                  pl.BlockSpec((tm, tn), lambda j, i: (i, j))],
        out_specs=pl.BlockSpec((tm, tn), lambda j, i: (i, j)),
        out_shape=jax.ShapeDtypeStruct((t, d), F32),
        compiler_params=_params("parallel", "parallel"),
        name="outproj",
    )(*ys, w, x)


def _gateup_body(a_ref, wg_ref, wu_ref, cw_ref, cb_ref, o_ref, gbuf, *, tiles_per_seq):
    i = pl.program_id(1)
    tm = a_ref.shape[0]
    a = a_ref[...]
    g = _dot(a, wg_ref[0])
    u = _dot(a, wu_ref[0])

    @pl.when(i % tiles_per_seq == 0)
    def _():
        gbuf[0:SUBLANES, :] = jnp.zeros((SUBLANES, gbuf.shape[1]), F32)

    gbuf[SUBLANES:SUBLANES + tm, :] = g
    cw = cw_ref[0]
    gt = (cb_ref[0] + cw[2:3, :] * g
          + cw[1:2, :] * gbuf[SUBLANES - 1:SUBLANES - 1 + tm, :]
          + cw[0:1, :] * gbuf[SUBLANES - 2:SUBLANES - 2 + tm, :])
    o_ref[...] = (gt * _sigmoid(gt) * u).astype(o_ref.dtype)
    gbuf[0:SUBLANES, :] = g[tm - SUBLANES:tm, :]


def _gateup(a, wg, wu, cw, cb3, layer, seq, tm, tn):
    t, k = a.shape
    fp = wg.shape[2]
    w_spec = pl.BlockSpec((1, k, tn), lambda j, i: (layer, 0, j))
    return pl.pallas_call(
        functools.partial(_gateup_body, tiles_per_seq=seq // tm),
        grid=(fp // tn, t // tm),
        in_specs=[pl.BlockSpec((tm, k), lambda j, i: (i, 0)), w_spec, w_spec,
                  pl.BlockSpec((1, MLP_CONV, tn), lambda j, i: (layer, 0, j)),
                  pl.BlockSpec((1, 1, tn), lambda j, i: (layer, 0, j))],
        out_specs=pl.BlockSpec((tm, tn), lambda j, i: (i, j)),
        out_shape=jax.ShapeDtypeStruct((t, fp), BF16),
        scratch_shapes=[pltpu.VMEM((SUBLANES + tm, tn), F32)],
        compiler_params=_params("parallel", "arbitrary"),
        name="mlp_gateup",
    )(a, wg, wu, cw, cb3)


def _down_body(h_ref, w_ref, x_ref, o_ref):
    @pl.when(pl.program_id(2) == 0)
    def _():
        o_ref[...] = x_ref[...]

    o_ref[...] += _dot(h_ref[...], w_ref[0])


def _down(h, w, x, layer, tm, tn, tk):
    t, fp = h.shape
    d = x.shape[1]
    return pl.pallas_call(
        _down_body,
        grid=(d // tn, t // tm, fp // tk),
        in_specs=[pl.BlockSpec((tm, tk), lambda j, i, k: (i, k)),
                  pl.BlockSpec((1, tk, tn), lambda j, i, k: (layer, k, j)),
                  pl.BlockSpec((tm, tn), lambda j, i, k: (i, j))],
        out_specs=pl.BlockSpec((tm, tn), lambda j, i, k: (i, j)),
        out_shape=jax.ShapeDtypeStruct((t, d), F32),
        compiler_params=_params("parallel", "parallel", "arbitrary"),
        name="mlp_down",
    )(h, w, x)


def _retention_body(q_ref, k_ref, v_ref, g_ref, cos_ref, sin_ref, inner_ref, qd_ref, kd_ref, cd_ref,
                    o_ref, state, *, heads):
    @pl.when(pl.program_id(1) == 0)
    def _():
        state[...] = jnp.zeros(state.shape, F32)

    cos = cos_ref[...]
    sin = sin_ref[...]

    def rot(x):
        return x * cos + pltpu.roll(x, HEAD_DIM // 2, axis=1) * sin

    for h in range(heads):
        cols = slice(h * HEAD_DIM, (h + 1) * HEAD_DIM)
        q = rot(q_ref[:, cols])
        k = rot(k_ref[:, cols]) * (HEAD_DIM ** -0.5)
        vb = v_ref[:, cols].astype(BF16)
        qb = q.astype(BF16)
        inner = _dot_nt(qb, k.astype(BF16)) * inner_ref[h]
        st = state[h]
        o = _dot(inner.astype(BF16), vb) + _dot(qb, st.astype(BF16)) * qd_ref[h]
        kd = (k * kd_ref[h]).T.astype(BF16)
        state[h] = st * cd_ref[h] + _dot(kd, vb)
        g = g_ref[:, cols]
        o_ref[:, cols] = (_rms_rows(o) * (g * _sigmoid(g))).astype(o_ref.dtype)


def _retention(proj, seq, batch, gw):
    t = proj.shape[0]
    heads = gw // HEAD_DIM
    c = RET_CHUNK
    nc = seq // c
    pos = jnp.arange(seq)
    inv = 1.0 / (RET_ROT_BASE ** jnp.linspace(0.0, 1.0, HEAD_DIM // 2, dtype=F32))
    ang = pos.astype(F32)[:, None] * inv[None, :]
    cos, sin = jnp.cos(ang), jnp.sin(ang)
    cos_t = jnp.concatenate([cos, cos], axis=1)
    sin_t = jnp.concatenate([-sin, sin], axis=1)
    log_g = jnp.log(1.0 - 2.0 ** (-5.0 - jnp.arange(heads, dtype=F32)))
    idx = jnp.arange(c, dtype=F32)
    rel = idx[:, None] - idx[None, :]
    inner_decay = jnp.where(rel >= 0, jnp.exp(log_g[:, None, None] * jnp.maximum(rel, 0.0)), 0.0)
    q_decay = jnp.broadcast_to(jnp.exp(log_g[:, None] * (idx + 1.0))[..., None], (heads, c, HEAD_DIM))
    k_decay = jnp.broadcast_to(jnp.exp(log_g[:, None] * (c - 1.0 - idx))[..., None], (heads, c, HEAD_DIM))
    chunk_decay = jnp.broadcast_to(jnp.exp(log_g * c)[:, None, None], (heads, HEAD_DIM, HEAD_DIM))

    def col(n):
        return pl.BlockSpec((c, gw), lambda b, i: (b * nc + i, n))

    rope_spec = pl.BlockSpec((c, HEAD_DIM), lambda b, i: (i, 0))
    const_spec = pl.BlockSpec((heads, c, HEAD_DIM), lambda b, i: (0, 0, 0))
    return pl.pallas_call(
        functools.partial(_retention_body, heads=heads),
        grid=(batch, nc),
        in_specs=[col(0), col(1), col(2), col(3), rope_spec, rope_spec,
                  const_spec, const_spec, const_spec, const_spec],
        out_specs=pl.BlockSpec((c, gw), lambda b, i: (b * nc + i, 0)),
        out_shape=jax.ShapeDtypeStruct((t, gw), BF16),
        scratch_shapes=[pltpu.VMEM((heads, HEAD_DIM, HEAD_DIM), F32)],
        compiler_params=_params("parallel", "arbitrary"),
        name="retention",
    )(proj, proj, proj, proj, cos_t, sin_t, inner_decay, q_decay, k_decay, chunk_decay)


def _stickbreak_body(q_ref, k_ref, v_ref, tri_ref, o_ref, *, blk):
    i = pl.program_id(2)
    q = (q_ref[...] * (HEAD_DIM ** -0.5)).astype(BF16)
    tri = tri_ref[...]
    row = lax.broadcasted_iota(jnp.int32, (blk, blk), 0)
    col = lax.broadcasted_iota(jnp.int32, (blk, blk), 1)
    past = col < row

    def tile(j, carry, acc, masked):
        start = pl.multiple_of(j * blk, blk)
        kb = k_ref[pl.ds(start, blk), :].astype(BF16)
        vb = v_ref[pl.ds(start, blk), :].astype(BF16)
        z = _dot_nt(q, kb)
        log_stay = -_softplus(z)
        if masked:
            log_stay = jnp.where(past, log_stay, 0.0)
        cum = _dot_split(log_stay, tri) + carry
        w = jnp.exp(z + cum)
        if masked:
            w = jnp.where(past, w, 0.0)
        acc = acc + _dot(w.astype(BF16), vb)
        carry = carry + jnp.sum(log_stay, axis=1, keepdims=True)
        return carry, acc

    carry, acc = tile(i, jnp.zeros((blk, 1), F32), jnp.zeros((blk, HEAD_DIM), F32), True)

    def body(jj, c):
        return tile(i - 1 - jj, c[0], c[1], False)

    carry, acc = lax.fori_loop(0, i, body, (carry, acc))
    o_ref[...] = acc.astype(o_ref.dtype)


def _stickbreak(proj, seq, batch, gw, col0):
    t = proj.shape[0]
    heads = gw // HEAD_DIM
    blk = 256
    nq = seq // blk
    ii = np.arange(blk)
    tri = jnp.asarray(ii[:, None] >= ii[None, :], BF16)
    kv_spec = lambda n: pl.BlockSpec((seq, HEAD_DIM), lambda b, h, i: (b, col0 + n * heads + h))
    return pl.pallas_call(
        functools.partial(_stickbreak_body, blk=blk),
        grid=(batch, heads, nq),
        in_specs=[pl.BlockSpec((blk, HEAD_DIM), lambda b, h, i: (b * nq + i, col0 + h)),
                  kv_spec(1), kv_spec(2),
                  pl.BlockSpec((blk, blk), lambda b, h, i: (0, 0))],
        out_specs=pl.BlockSpec((blk, HEAD_DIM), lambda b, h, i: (b * nq + i, h)),
        out_shape=jax.ShapeDtypeStruct((t, gw), BF16),
        compiler_params=_params("parallel", "parallel", "arbitrary"),
        name="stickbreak",
    )(proj, proj, proj, tri)


def _rglru_body(gate_ref, rec_ref, cw_ref, cb_ref, wa_ref, ba_ref, wx_ref, bx_ref, lam_ref, o_ref,
                xbuf, abuf, ubuf, hprev, *, blocks):
    i = pl.program_id(1)
    ts = rec_ref.shape[0]
    gw = rec_ref.shape[1]

    @pl.when(i == 0)
    def _():
        xbuf[0:SUBLANES, :] = jnp.zeros((SUBLANES, gw), F32)
        hprev[...] = jnp.zeros(hprev.shape, F32)

    x = rec_ref[...]
    xbuf[SUBLANES:SUBLANES + ts, :] = x
    cw = cw_ref[0]
    xr = cb_ref[0] + cw[3:4, :] * x
    for k in range(LRU_CONV - 1):
        off = SUBLANES - (LRU_CONV - 1) + k
        xr = xr + cw[k:k + 1, :] * xbuf[off:off + ts, :]
    xbuf[0:SUBLANES, :] = x[ts - SUBLANES:ts, :]

    log_sig_lam = -_softplus(-lam_ref[0])
    xrb = xr.astype(BF16)
    bw = gw // blocks
    first = (lax.broadcasted_iota(jnp.int32, (ts, bw), 0) == 0) & (i == 0)
    for n in range(blocks):
        cols = slice(n * bw, (n + 1) * bw)
        r = _sigmoid(_dot(xrb[:, cols], wa_ref[0, n]) + ba_ref[0, :, cols])
        gi = _sigmoid(_dot(xrb[:, cols], wx_ref[0, n]) + bx_ref[0, :, cols])
        log_a = LRU_C * r * log_sig_lam[:, cols]
        mult = jnp.where(first, 1.0, jnp.sqrt(jnp.maximum(1.0 - jnp.exp(2.0 * log_a), 0.0)))
        abuf[:, cols] = jnp.exp(log_a)
        ubuf[:, cols] = mult * (gi * xr[:, cols])

    rows = lax.broadcasted_iota(jnp.int32, (SUBLANES, gw), 0)

    def group(gidx, h):
        start = pl.multiple_of(gidx * SUBLANES, SUBLANES)
        a = abuf[pl.ds(start, SUBLANES), :]
        u = ubuf[pl.ds(start, SUBLANES), :]
        for s in (1, 2, 4):
            keep = rows >= s
            a_sh = jnp.where(keep, pltpu.roll(a, s, axis=0), 1.0)
            u_sh = jnp.where(keep, pltpu.roll(u, s, axis=0), 0.0)
            u = u + a * u_sh
            a = a * a_sh
        hs = u + a * h
        ubuf[pl.ds(start, SUBLANES), :] = hs
        return jnp.broadcast_to(hs[SUBLANES - 1:SUBLANES, :], (SUBLANES, gw))

    hprev[...] = lax.fori_loop(0, ts // SUBLANES, group, hprev[...])
    o_ref[...] = (ubuf[...] * _gelu_tanh(gate_ref[...])).astype(o_ref.dtype)


def _rglru(proj, cw, cb3, wa, ba3, wx, bx3, lam3, layer, seq, batch, gw, col0):
    t = proj.shape[0]
    blocks = gw // HEAD_DIM
    ts = 256
    ns = seq // ts
    cpb = col0 // blocks
    vec_spec = pl.BlockSpec((1, 1, gw), lambda b, i: (layer, 0, 0))
    w_spec = pl.BlockSpec((1, blocks, HEAD_DIM, HEAD_DIM), lambda b, i: (layer, 0, 0, 0))
    return pl.pallas_call(
        functools.partial(_rglru_body, blocks=blocks),
        grid=(batch, ns),
        in_specs=[pl.BlockSpec((ts, gw), lambda b, i: (b * ns + i, cpb)),
                  pl.BlockSpec((ts, gw), lambda b, i: (b * ns + i, cpb + 1)),
                  pl.BlockSpec((1, LRU_CONV, gw), lambda b, i: (layer, 0, 0)),
                  vec_spec, w_spec, vec_spec, w_spec, vec_spec, vec_spec],
        out_specs=pl.BlockSpec((ts, gw), lambda b, i: (b * ns + i, 0)),
        out_shape=jax.ShapeDtypeStruct((t, gw), BF16),
        scratch_shapes=[pltpu.VMEM((SUBLANES + ts, gw), F32), pltpu.VMEM((ts, gw), F32),
                        pltpu.VMEM((ts, gw), F32), pltpu.VMEM((SUBLANES, gw), F32)],
        compiler_params=_params("parallel", "arbitrary"),
        name="rglru",
    )(proj, proj, cw, cb3, wa, ba3, wx, bx3, lam3)


def _rope_tables(pos):
    half = ROPE_DIMS // 2
    inv = ROPE_THETA ** (-jnp.arange(half, dtype=F32) / half)
    ang = pos.astype(F32)[:, None] * inv[None, :]
    cos, sin = jnp.cos(ang), jnp.sin(ang)
    n = pos.shape[0]
    zeros = jnp.zeros((n, half), F32)
    rest = HEAD_DIM - 2 * half
    c = jnp.concatenate([cos, cos, jnp.ones((n, rest), F32)], axis=1)
    s_lo = jnp.concatenate([-sin, zeros, jnp.zeros((n, rest), F32)], axis=1)
    s_hi = jnp.concatenate([zeros, sin, jnp.zeros((n, rest), F32)], axis=1)
    return c, s_lo, s_hi


def _partial_rope(x, c, s_lo, s_hi):
    half = ROPE_DIMS // 2
    return x * c + pltpu.roll(x, HEAD_DIM - half, axis=1) * s_lo + pltpu.roll(x, half, axis=1) * s_hi


def _nsa_prep_body(q_ref, ks_ref, kw_ref, qw_ref, kwt_ref, c_ref, lo_ref, hi_ref, qo_ref, ko_ref,
                   *, heads, kv_heads):
    c, s_lo, s_hi = c_ref[...], lo_ref[...], hi_ref[...]

    def prep(x, w):
        return _partial_rope(_rms_rows(x) * w, c, s_lo, s_hi)

    qw = qw_ref[0]
    for h in range(heads):
        cols = slice(h * HEAD_DIM, (h + 1) * HEAD_DIM)
        qo_ref[:, cols] = (prep(q_ref[:, cols], qw) * (HEAD_DIM ** -0.5)).astype(qo_ref.dtype)
    for n, src in enumerate((ks_ref, kw_ref)):
        w = kwt_ref[0, n + 1:n + 2, :]
        for g in range(kv_heads):
            cols = slice(g * HEAD_DIM, (g + 1) * HEAD_DIM)
            dst = slice((n * kv_heads + g) * HEAD_DIM, (n * kv_heads + g + 1) * HEAD_DIM)
            ko_ref[:, dst] = prep(src[:, cols], w).astype(ko_ref.dtype)


def _nsa_prep(proj, qw3, kw3, tables, layer, seq, batch, gw, colq, colkv):
    t = proj.shape[0]
    heads = gw // HEAD_DIM
    kvw = NSA_KV_HEADS * HEAD_DIM
    ts = 256
    ns = seq // ts
    rope_spec = pl.BlockSpec((ts, HEAD_DIM), lambda b, i: (i, 0))
    cq = colq * HEAD_DIM // gw
    ck = colkv * HEAD_DIM // kvw
    return pl.pallas_call(
        functools.partial(_nsa_prep_body, heads=heads, kv_heads=NSA_KV_HEADS),
        grid=(batch, ns),
        in_specs=[pl.BlockSpec((ts, gw), lambda b, i: (b * ns + i, cq)),
                  pl.BlockSpec((ts, kvw), lambda b, i: (b * ns + i, ck + 2)),
                  pl.BlockSpec((ts, kvw), lambda b, i: (b * ns + i, ck + 4)),
                  pl.BlockSpec((1, 1, HEAD_DIM), lambda b, i: (layer, 0, 0)),
                  pl.BlockSpec((1, 3, HEAD_DIM), lambda b, i: (layer, 0, 0)),
                  rope_spec, rope_spec, rope_spec],
        out_specs=[pl.BlockSpec((ts, gw), lambda b, i: (b * ns + i, 0)),
                   pl.BlockSpec((ts, 2 * kvw), lambda b, i: (b * ns + i, 0))],
        out_shape=[jax.ShapeDtypeStruct((t, gw), BF16), jax.ShapeDtypeStruct((t, 2 * kvw), BF16)],
        compiler_params=_params("parallel", "parallel"),
        name="nsa_prep",
    )(proj, proj, proj, qw3, kw3, *tables)


def _compress_body(seg_ref, pos_ref, w1_ref, w2_ref, kw_ref, c_ref, lo_ref, hi_ref, o_ref):
    seg = seg_ref[0, 0]
    ns = seg.shape[0]
    first = _dot((seg + pos_ref[0, 0, 0:1, :]).astype(BF16), w1_ref[0, 0, 0])
    second = _dot((seg + pos_ref[0, 0, 1:2, :]).astype(BF16), w1_ref[0, 0, 1])
    hid = _gelu_tanh(first + pltpu.roll(second, ns - 1, axis=0))
    out = _dot(hid.astype(BF16), w2_ref[0, 0])

    @pl.when(pl.program_id(1) < NSA_KV_HEADS)
    def _():
        o_ref[0, 0] = _partial_rope(_rms_rows(out) * kw_ref[0, 0:1, :], c_ref[...], lo_ref[...], hi_ref[...])

    @pl.when(pl.program_id(1) >= NSA_KV_HEADS)
    def _():
        o_ref[0, 0] = out


def _compress(segs, pos4, w1, w2, kw3, tables, layer):
    batch, n4, ns, width = segs.shape
    hidden = w1.shape[-1]
    g = NSA_KV_HEADS
    full = pl.BlockSpec((ns, HEAD_DIM), lambda b, c: (0, 0))
    return pl.pallas_call(
        _compress_body,
        grid=(batch, n4),
        in_specs=[pl.BlockSpec((1, 1, ns, width), lambda b, c: (b, c, 0, 0)),
                  pl.BlockSpec((1, 1, 2, width), lambda b, c: (layer, c // g, 0, 0)),
                  pl.BlockSpec((1, 1, 2, width, hidden), lambda b, c: (layer, c // g, 0, 0, 0)),
                  pl.BlockSpec((1, 1, hidden, HEAD_DIM), lambda b, c: (layer, c // g, 0, 0)),
                  pl.BlockSpec((1, 3, HEAD_DIM), lambda b, c: (layer, 0, 0)),
                  full, full, full],
        out_specs=pl.BlockSpec((1, 1, ns, HEAD_DIM), lambda b, c: (b, c, 0, 0)),
        out_shape=jax.ShapeDtypeStruct((batch, n4, ns, HEAD_DIM), F32),
        compiler_params=_params("parallel", "parallel"),
        name="nsa_compress",
    )(segs, pos4, w1, w2, kw3, *tables)


def _cmp_select_body(q_ref, kc_ref, vc_ref, ov_ref, o_ref, sel_ref, score_t, *, rep, n_cmp, n_slc, top_k):
    i = pl.program_id(2)
    tq = q_ref.shape[0]
    ncp = kc_ref.shape[2]
    nsp = ov_ref.shape[1]
    kc = kc_ref[0, 0].astype(BF16)
    vc = vc_ref[0, 0].astype(BF16)
    tpos = i * tq + lax.broadcasted_iota(jnp.int32, (tq, ncp), 0)
    blk_n = lax.broadcasted_iota(jnp.int32, (tq, ncp), 1)
    visible = (blk_n * CMP_STRIDE + (CMP_LEN - 1) <= tpos) & (blk_n < n_cmp)
    p_sum = jnp.zeros((tq, ncp), F32)
    for r in range(rep):
        cols = slice(r * HEAD_DIM, (r + 1) * HEAD_DIM)
        s = jnp.where(visible, _dot_nt(q_ref[:, cols], kc), -jnp.inf)
        m = jnp.max(s, axis=-1, keepdims=True)
        m = jnp.where(m > -jnp.inf, m, 0.0)
        e = jnp.where(visible, jnp.exp(s - m), 0.0)
        p = e / jnp.maximum(jnp.sum(e, axis=-1, keepdims=True), 1e-30)
        o_ref[:, cols] = _dot(p.astype(BF16), vc)
        p_sum = p_sum + p
    p_slc = _dot_split(p_sum, ov_ref[...])

    tpos_s = i * tq + lax.broadcasted_iota(jnp.int32, (tq, nsp), 0)
    blk_s = lax.broadcasted_iota(jnp.int32, (tq, nsp), 1)
    cur = tpos_s // SLC_LEN
    forced = (blk_s == 0) | (blk_s == cur) | (blk_s == cur - 1)
    valid = blk_s <= cur
    score = jnp.where(forced, FORCED_SCORE, jnp.where(valid, p_slc, NEG_BIG))
    score_t[...] = score.T
    mine = score_t[...]
    my_blk = lax.broadcasted_iota(jnp.int32, (nsp, tq), 0)

    def count(other, rank):
        row = score_t[pl.ds(other, 1), :]
        ahead = (row > mine) | ((row == mine) & (other < my_blk))
        return rank + jnp.where(ahead, 1.0, 0.0)

    rank = lax.fori_loop(0, n_slc, count, jnp.zeros((nsp, tq), F32))
    chosen = jnp.where(rank < top_k, 1.0, 0.0).T
    sel_ref[0] = jnp.where(valid, chosen, 0.0).astype(sel_ref.dtype)


def _cmp_select(qn, cmp_kv, overlap, seq, batch, gw):
    t = qn.shape[0]
    g = NSA_KV_HEADS
    rep = gw // HEAD_DIM // g
    ncp = cmp_kv.shape[2]
    nsp = overlap.shape[1]
    n_slc = seq // SLC_LEN
    tq = 256
    nq = seq // tq
    body = functools.partial(_cmp_select_body, rep=rep, n_cmp=(seq - CMP_LEN) // CMP_STRIDE + 1,
                             n_slc=n_slc, top_k=min(SLC_TOPK, n_slc))
    return pl.pallas_call(
        body,
        grid=(batch, g, nq),
        in_specs=[pl.BlockSpec((tq, rep * HEAD_DIM), lambda b, gi, i: (b * nq + i, gi)),
                  pl.BlockSpec((1, 1, ncp, HEAD_DIM), lambda b, gi, i: (b, gi, 0, 0)),
                  pl.BlockSpec((1, 1, ncp, HEAD_DIM), lambda b, gi, i: (b, g + gi, 0, 0)),
                  pl.BlockSpec((ncp, nsp), lambda b, gi, i: (0, 0))],
        out_specs=[pl.BlockSpec((tq, rep * HEAD_DIM), lambda b, gi, i: (b * nq + i, gi)),
                   pl.BlockSpec((1, tq, nsp), lambda b, gi, i: (gi, b * nq + i, 0))],
        out_shape=[jax.ShapeDtypeStruct((t, gw), F32), jax.ShapeDtypeStruct((g, t, nsp), BF16)],
        scratch_shapes=[pltpu.VMEM((nsp, tq), F32)],
        compiler_params=_params("parallel", "parallel", "parallel"),
        name="nsa_cmp_select",
    )(qn, cmp_kv, cmp_kv, overlap)


def _slc_body(q_ref, k_ref, v_ref, sel_ref, ex_ref, o_ref, *, rep, blk):
    i = pl.program_id(2)
    tq = q_ref.shape[0]
    q = jnp.concatenate([q_ref[:, r * HEAD_DIM:(r + 1) * HEAD_DIM] for r in range(rep)], axis=0)
    sel = sel_ref[0]
    row = lax.broadcasted_iota(jnp.int32, (rep * tq, blk), 0)
    tpos = i * tq + (row & (tq - 1))
    kcol = lax.broadcasted_iota(jnp.int32, (rep * tq, blk), 1)

    def step(j, carry):
        m, l, acc = carry
        start = pl.multiple_of(j * blk, blk)
        kb = k_ref[pl.ds(start, blk), :]
        vb = v_ref[pl.ds(start, blk), :].astype(BF16)
        picked = _dot(sel, ex_ref[j])
        picked = jnp.concatenate([picked] * rep, axis=0)
        mask = (picked > 0.5) & (start + kcol <= tpos)
        s = jnp.where(mask, _dot_nt(q, kb), NEG_BIG)
        m_new = jnp.maximum(m, jnp.max(s, axis=-1, keepdims=True))
        alpha = jnp.exp(m - m_new)
        p = jnp.where(mask, jnp.exp(s - m_new), 0.0)
        l = alpha * l + jnp.sum(p, axis=-1, keepdims=True)
        acc = alpha * acc + _dot(p.astype(BF16), vb)
        return m_new, l, acc

    n_blocks = ((i + 1) * tq + blk - 1) // blk
    init = (jnp.full((rep * tq, 1), NEG_BIG, F32), jnp.zeros((rep * tq, 1), F32),
            jnp.zeros((rep * tq, HEAD_DIM), F32))
    m, l, acc = lax.fori_loop(0, n_blocks, step, init)
    out = acc / l
    for r in range(rep):
        o_ref[:, r * HEAD_DIM:(r + 1) * HEAD_DIM] = out[r * tq:(r + 1) * tq, :]


def _slc_attention(qn, kk, proj, sel, expand, seq, batch, gw, colv):
    t = qn.shape[0]
    g = NSA_KV_HEADS
    rep = gw // HEAD_DIM // g
    nsp = sel.shape[2]
    tq = 128
    blk = expand.shape[2]
    nq = seq // tq
    return pl.pallas_call(
        functools.partial(_slc_body, rep=rep, blk=blk),
        grid=(batch, g, nq),
        in_specs=[pl.BlockSpec((tq, rep * HEAD_DIM), lambda b, gi, i: (b * nq + i, gi)),
                  pl.BlockSpec((seq, HEAD_DIM), lambda b, gi, i: (b, gi)),
                  pl.BlockSpec((seq, HEAD_DIM), lambda b, gi, i: (b, colv + gi)),
                  pl.BlockSpec((1, tq, nsp), lambda b, gi, i: (gi, b * nq + i, 0)),
                  pl.BlockSpec((seq // blk, nsp, blk), lambda b, gi, i: (0, 0, 0))],
        out_specs=pl.BlockSpec((tq, rep * HEAD_DIM), lambda b, gi, i: (b * nq + i, gi)),
        out_shape=jax.ShapeDtypeStruct((t, gw), F32),
        compiler_params=_params("parallel", "parallel", "arbitrary"),
        name="nsa_slc",
    )(qn, kk, proj, sel, expand)


def _win_combine_body(q_ref, k_ref, v_ref, oc_ref, os_ref, gl_ref, o_ref, *, rep, span):
    i = pl.program_id(2)
    gi = pl.program_id(1)
    tq = q_ref.shape[0]
    q = jnp.concatenate([q_ref[:, r * HEAD_DIM:(r + 1) * HEAD_DIM] for r in range(rep)], axis=0)
    start = pl.multiple_of(jnp.maximum(i * tq + tq - span, 0), tq)
    kb = k_ref[pl.ds(start, span), :]
    vb = v_ref[pl.ds(start, span), :].astype(BF16)
    tpos = i * tq + (lax.broadcasted_iota(jnp.int32, (rep * tq, span), 0) & (tq - 1))
    dist = tpos - (start + lax.broadcasted_iota(jnp.int32, (rep * tq, span), 1))
    mask = (dist >= 0) & (dist < WINDOW)
    s = jnp.where(mask, _dot_nt(q, kb), NEG_BIG)
    m = jnp.max(s, axis=-1, keepdims=True)
    e = jnp.where(mask, jnp.exp(s - m), 0.0)
    o_win = _dot(e.astype(BF16), vb) / jnp.sum(e, axis=-1, keepdims=True)
    gates = _sigmoid(gl_ref[...])
    lane = lax.broadcasted_iota(jnp.int32, gates.shape, 1)
    for r in range(rep):
        cols = slice(r * HEAD_DIM, (r + 1) * HEAD_DIM)
        head = gi * rep + r

        def gate(branch):
            return jnp.sum(jnp.where(lane == 3 * head + branch, gates, 0.0), axis=-1, keepdims=True)

        mix = gate(0) * oc_ref[:, cols] + gate(1) * os_ref[:, cols] + gate(2) * o_win[r * tq:(r + 1) * tq, :]
        o_ref[:, cols] = mix.astype(o_ref.dtype)


def _win_combine(qn, kk, proj, o_cmp, o_slc, gl, seq, batch, gw, colv):
    t = qn.shape[0]
    g = NSA_KV_HEADS
    rep = gw // HEAD_DIM // g
    tq = 128
    nq = seq // tq
    span = WINDOW + tq
    head_spec = pl.BlockSpec((tq, rep * HEAD_DIM), lambda b, gi, i: (b * nq + i, gi))
    return pl.pallas_call(
        functools.partial(_win_combine_body, rep=rep, span=span),
        grid=(batch, g, nq),
        in_specs=[head_spec,
                  pl.BlockSpec((seq, HEAD_DIM), lambda b, gi, i: (b, g + gi)),
                  pl.BlockSpec((seq, HEAD_DIM), lambda b, gi, i: (b, colv + gi)),
                  head_spec, head_spec,
                  pl.BlockSpec((tq, LANES), lambda b, gi, i: (b * nq + i, 0))],
        out_specs=head_spec,
        out_shape=jax.ShapeDtypeStruct((t, gw), BF16),
        compiler_params=_params("parallel", "parallel", "parallel"),
        name="nsa_win_combine",
    )(qn, kk, proj, o_cmp, o_slc, gl)


def _nsa(proj, gl, qw3, kw3, pos4, w1, w2, layer, seq, batch, gw, colq, colkv):
    g = NSA_KV_HEADS
    kvw = g * HEAD_DIM
    tables = _rope_tables(jnp.arange(seq))
    qn, kk = _nsa_prep(proj, qw3, kw3, tables, layer, seq, batch, gw, colq, colkv)

    ns = seq // CMP_STRIDE
    n_cmp = (seq - CMP_LEN) // CMP_STRIDE + 1
    c0 = colkv * HEAD_DIM
    segs = proj[:, c0:c0 + 2 * kvw].reshape(batch, ns, CMP_STRIDE, 2 * g, HEAD_DIM)
    segs = segs.transpose(0, 3, 1, 2, 4).reshape(batch, 2 * g, ns, CMP_STRIDE * HEAD_DIM)
    cmp_end = jnp.arange(ns) * CMP_STRIDE + CMP_LEN - 1
    cmp_kv = _compress(segs, pos4, w1, w2, kw3, _rope_tables(cmp_end), layer)

    n_slc = seq // SLC_LEN
    nsp = _round_up(n_slc, LANES)
    ci = np.arange(ns)[:, None] * CMP_STRIDE
    sj = np.arange(nsp)[None, :] * SLC_LEN
    ov = np.maximum(0, np.minimum(ci + CMP_LEN, sj + SLC_LEN) - np.maximum(ci, sj)) / CMP_STRIDE
    ov = np.where((np.arange(ns)[:, None] < n_cmp) & (np.arange(nsp)[None, :] < n_slc), ov, 0.0)
    o_cmp, sel = _cmp_select(qn, cmp_kv, jnp.asarray(ov, BF16), seq, batch, gw)

    blk = 256
    key_blk = (np.arange(seq) // SLC_LEN).reshape(seq // blk, 1, blk)
    expand = jnp.asarray(np.arange(nsp)[None, :, None] == key_blk, BF16)
    colv = colkv + 2 * g
    o_slc = _slc_attention(qn, kk, proj, sel, expand, seq, batch, gw, colv + g)
    return _win_combine(qn, kk, proj, o_cmp, o_slc, gl, seq, batch, gw, colv + 3 * g)


def kernel(x, attn_norm_w, w_in, lru_conv_w, lru_conv_b, lru_w_a, lru_b_a, lru_w_x, lru_b_x, lru_lambda,
           nsa_q_norm_w, nsa_k_norm_w, nsa_cmp_pos, nsa_cmp_w1, nsa_cmp_w2, w_out, mlp_norm_w,
           w_gate, w_up, mlp_conv_w, mlp_conv_b, w_down):
    batch, seq, d = x.shape
    depth = w_in.shape[0]
    gw = d // N_MIXERS
    gh = gw // HEAD_DIM
    kvw = NSA_KV_HEADS * HEAD_DIM
    n_main = 10 * gw + 6 * kvw
    n_gate = w_in.shape[2] - n_main
    f = w_gate.shape[2]
    fp = _round_up(f, 1024)
    t = batch * seq
    assert seq % 256 == 0 and seq >= WINDOW + 128 and gw % (NSA_KV_HEADS * HEAD_DIM) == 0
    assert n_main % 512 == 0 and n_gate <= LANES and d % 512 == 0

    w_main = w_in[:, :, :n_main].astype(BF16)
    w_gl = jnp.pad(w_in[:, :, n_main:], ((0, 0), (0, 0), (0, LANES - n_gate))).astype(BF16)
    w_out_b = w_out.astype(BF16)
    pad_f = ((0, 0), (0, 0), (0, fp - f))
    w_gate_b = jnp.pad(w_gate, pad_f).astype(BF16)
    w_up_b = jnp.pad(w_up, pad_f).astype(BF16)
    w_down_b = jnp.pad(w_down, ((0, 0), (0, fp - f), (0, 0))).astype(BF16)
    mlp_cw = jnp.pad(mlp_conv_w, pad_f)
    mlp_cb3 = jnp.pad(mlp_conv_b, ((0, 0), (0, fp - f)))[:, None, :]
    attn_w3 = attn_norm_w[:, None, :]
    mlp_w3 = mlp_norm_w[:, None, :]
    lru_cb3, lru_ba3, lru_bx3, lru_lam3 = (v[:, None, :] for v in (lru_conv_b, lru_b_a, lru_b_x, lru_lambda))
    lru_wa_b = lru_w_a.astype(BF16)
    lru_wx_b = lru_w_x.astype(BF16)
    qw3 = nsa_q_norm_w[:, None, :]
    seg_w = CMP_STRIDE * HEAD_DIM
    pos4 = nsa_cmp_pos.reshape(depth, 2, CMP_LEN // CMP_STRIDE, seg_w)
    cmp_w1 = nsa_cmp_w1.reshape(depth, 2, CMP_LEN // CMP_STRIDE, seg_w, -1).astype(BF16)
    cmp_w2 = nsa_cmp_w2.astype(BF16)

    xf = x.reshape(t, d)
    for layer in range(depth):
        hn = _rmsnorm(xf, attn_w3, layer)
        proj = _matmul(hn, w_main, layer, 512, 512, F32, "inproj")
        gl = _matmul(hn, w_gl, layer, 512, LANES, F32, "inproj_gates")
        y_ret = _retention(proj, seq, batch, gw)
        y_sb = _stickbreak(proj, seq, batch, gw, 4 * gh)
        y_lru = _rglru(proj, lru_conv_w, lru_cb3, lru_wa_b, lru_ba3, lru_wx_b, lru_bx3, lru_lam3,
                       layer, seq, batch, gw, 7 * gh)
        y_nsa = _nsa(proj, gl, qw3, nsa_k_norm_w, pos4, cmp_w1, cmp_w2, layer, seq, batch, gw,
                     9 * gh, 10 * gh)
        xf = _outproj((y_ret, y_sb, y_lru, y_nsa), w_out_b, xf, layer, 512, 512)
        hn = _rmsnorm(xf, mlp_w3, layer)
        hid = _gateup(hn, w_gate_b, w_up_b, mlp_cw, mlp_cb3, layer, seq, 512, 512)
        xf = _down(hid, w_down_b, xf, layer, 512, 1024, 1024)
    return xf.reshape(batch, seq, d)
```

```python
import functools
import math

import numpy as np
import jax
import jax.numpy as jnp
from jax import lax
from jax.experimental import pallas as pl
from jax.experimental.pallas import tpu as pltpu

F32 = jnp.float32
BF16 = jnp.bfloat16

HEAD_DIM = 128
N_MIXERS = 4
RET_CHUNK = 128
RET_ROT_BASE = 10000.0
LRU_CONV = 4
LRU_C = 8.0
NSA_KV_HEADS = 2
CMP_LEN = 32
CMP_STRIDE = 16
SLC_LEN = 64
SLC_TOPK = 16
WINDOW = 512
ROPE_THETA = 500000.0
ROPE_DIMS = HEAD_DIM // 4
MLP_CONV = 3
NORM_EPS = 1e-6

LANES = 128
SUBLANES = 8
VMEM_LIMIT_BYTES = 56 * 1024 * 1024
NEG_BIG = -1e30
FORCED_SCORE = 1e30


def _params(*sem):
    return pltpu.CompilerParams(dimension_semantics=sem, vmem_limit_bytes=VMEM_LIMIT_BYTES)


def _round_up(x, m):
    return (x + m - 1) // m * m


def _dot(a, b):
    return jnp.dot(a, b, preferred_element_type=F32)


def _dot_nt(a, b):
    return lax.dot_general(a, b, (((1,), (1,)), ((), ())), preferred_element_type=F32)


def _dot_split(x, m):
    hi = x.astype(BF16)
    lo = (x - hi.astype(F32)).astype(BF16)
    return _dot(hi, m) + _dot(lo, m)


def _sigmoid(x):
    return 1.0 / (1.0 + jnp.exp(-x))


def _softplus(x):
    return jnp.maximum(x, 0.0) + jnp.log1p(jnp.exp(-jnp.abs(x)))


def _gelu_tanh(x):
    return x * (0.5 * (1.0 + jnp.tanh(math.sqrt(2.0 / math.pi) * (x + 0.044715 * (x * x * x)))))


def _rms_rows(x):
    return x * lax.rsqrt(jnp.mean(x * x, axis=-1, keepdims=True) + NORM_EPS)


def _rmsnorm_body(x_ref, w_ref, o_ref):
    o_ref[...] = (_rms_rows(x_ref[...]) * w_ref[0]).astype(o_ref.dtype)


def _rmsnorm(x, w3, layer):
    t, d = x.shape
    tr = 256
    return pl.pallas_call(
        _rmsnorm_body,
        grid=(t // tr,),
        in_specs=[pl.BlockSpec((tr, d), lambda i: (i, 0)),
                  pl.BlockSpec((1, 1, d), lambda i: (layer, 0, 0))],
        out_specs=pl.BlockSpec((tr, d), lambda i: (i, 0)),
        out_shape=jax.ShapeDtypeStruct((t, d), BF16),
        compiler_params=_params("parallel"),
        name="rmsnorm",
    )(x, w3)


def _matmul_body(a_ref, w_ref, o_ref):
    o_ref[...] = _dot(a_ref[...], w_ref[0]).astype(o_ref.dtype)


def _matmul(a, w, layer, tm, tn, out_dtype, name):
    t, k = a.shape
    n = w.shape[2]
    return pl.pallas_call(
        _matmul_body,
        grid=(t // tm, n // tn),
        in_specs=[pl.BlockSpec((tm, k), lambda i, j: (i, 0)),
                  pl.BlockSpec((1, k, tn), lambda i, j: (layer, 0, j))],
        out_specs=pl.BlockSpec((tm, tn), lambda i, j: (i, j)),
        out_shape=jax.ShapeDtypeStruct((t, n), out_dtype),
        compiler_params=_params("parallel", "parallel"),
        name=name,
    )(a, w)


def _outproj_body(y0_ref, y1_ref, y2_ref, y3_ref, w_ref, x_ref, o_ref):
    gw = y0_ref.shape[1]
    acc = x_ref[...]
    for n, y_ref in enumerate((y0_ref, y1_ref, y2_ref, y3_ref)):
        acc = acc + _dot(y_ref[...], w_ref[0, n * gw:(n + 1) * gw, :])
    o_ref[...] = acc


def _outproj(ys, w, x, layer, tm, tn):
    t, d = x.shape
    gw = ys[0].shape[1]
    y_spec = pl.BlockSpec((tm, gw), lambda j, i: (i, 0))
    return pl.pallas_call(
        _outproj_body,
        grid=(d // tn, t // tm),
        in_specs=[y_spec, y_spec, y_spec, y_spec,
                  pl.BlockSpec((1, N_MIXERS * gw, tn), lambda j, i: (layer, 0, j)),
                  pl.BlockSpec((tm, tn), lambda j, i: (i, j))],
        out_specs=pl.BlockSpec((tm, tn), lambda j, i: (i, j)),
        out_shape=jax.ShapeDtypeStruct((t, d), F32),
        compiler_params=_params("parallel", "parallel"),
        name="outproj",
    )(*ys, w, x)


def _gateup_body(a_ref, wg_ref, wu_ref, cw_ref, cb_ref, o_ref, gbuf, *, tiles_per_seq, sub):
    i = pl.program_id(1)
    tm = a_ref.shape[0]
    tn = o_ref.shape[1]
    a = a_ref[...]

    @pl.when(i % tiles_per_seq == 0)
    def _():
        gbuf[0:SUBLANES, :] = jnp.zeros((SUBLANES, tn), F32)

    for c in range(tn // sub):
        cols = slice(c * sub, (c + 1) * sub)
        g = _dot(a, wg_ref[0, :, cols])
        u = _dot(a, wu_ref[0, :, cols])
        gbuf[SUBLANES:SUBLANES + tm, cols] = g
        cw = cw_ref[0, :, cols]
        gt = (cb_ref[0, :, cols] + cw[2:3, :] * g
              + cw[1:2, :] * gbuf[SUBLANES - 1:SUBLANES - 1 + tm, cols]
              + cw[0:1, :] * gbuf[SUBLANES - 2:SUBLANES - 2 + tm, cols])
        o_ref[:, cols] = (gt * _sigmoid(gt) * u).astype(o_ref.dtype)
        gbuf[0:SUBLANES, cols] = g[tm - SUBLANES:tm, :]


def _gateup(a, wg, wu, cw, cb3, layer, seq, tm, tn):
    t, k = a.shape
    f = wg.shape[2]
    w_spec = pl.BlockSpec((1, k, tn), lambda j, i: (layer, 0, j))
    return pl.pallas_call(
        functools.partial(_gateup_body, tiles_per_seq=seq // tm, sub=2 * LANES),
        grid=(pl.cdiv(f, tn), t // tm),
        in_specs=[pl.BlockSpec((tm, k), lambda j, i: (i, 0)), w_spec, w_spec,
                  pl.BlockSpec((1, MLP_CONV, tn), lambda j, i: (layer, 0, j)),
                  pl.BlockSpec((1, 1, tn), lambda j, i: (layer, 0, j))],
        out_specs=pl.BlockSpec((tm, tn), lambda j, i: (i, j)),
        out_shape=jax.ShapeDtypeStruct((t, f), BF16),
        scratch_shapes=[pltpu.VMEM((SUBLANES + tm, tn), F32)],
        compiler_params=_params("parallel", "arbitrary"),
        name="mlp_gateup",
    )(a, wg, wu, cw, cb3)


def _down_body(h_ref, w_ref, x_ref, o_ref):
    o_ref[...] = x_ref[...] + _dot(h_ref[...], w_ref[0])


def _down(h, w, x, layer, tm, tn):
    t, f = h.shape
    d = x.shape[1]
    return pl.pallas_call(
        _down_body,
        grid=(d // tn, t // tm),
        in_specs=[pl.BlockSpec((tm, f), lambda j, i: (i, 0)),
                  pl.BlockSpec((1, f, tn), lambda j, i: (layer, 0, j), pipeline_mode=pl.Buffered(1)),
                  pl.BlockSpec((tm, tn), lambda j, i: (i, j))],
        out_specs=pl.BlockSpec((tm, tn), lambda j, i: (i, j)),
        out_shape=jax.ShapeDtypeStruct((t, d), F32),
        compiler_params=_params("parallel", "parallel"),
        name="mlp_down",
    )(h, w, x)


def _retention_body(q_ref, k_ref, v_ref, g_ref, cos_ref, sin_ref, inner_ref, qd_ref, kd_ref, cd_ref,
                    o_ref, state, *, heads):
    @pl.when(pl.program_id(1) == 0)
    def _():
        state[...] = jnp.zeros(state.shape, F32)

    cos = cos_ref[...]
    sin = sin_ref[...]

    def rot(x):
        return x * cos + pltpu.roll(x, HEAD_DIM // 2, axis=1) * sin

    for h in range(heads):
        cols = slice(h * HEAD_DIM, (h + 1) * HEAD_DIM)
        q = rot(q_ref[:, cols].astype(F32))
        k = rot(k_ref[:, cols].astype(F32)) * (HEAD_DIM ** -0.5)
        vb = v_ref[:, cols]
        qb = q.astype(BF16)
        inner = _dot_nt(qb, k.astype(BF16)) * inner_ref[h]
        st = state[h]
        o = _dot(inner.astype(BF16), vb) + _dot(qb, st.astype(BF16)) * qd_ref[h]
        kd = (k * kd_ref[h]).T.astype(BF16)
        state[h] = st * cd_ref[h] + _dot(kd, vb)
        g = g_ref[:, cols].astype(F32)
        o_ref[:, cols] = (_rms_rows(o) * (g * _sigmoid(g))).astype(o_ref.dtype)


def _retention(proj, seq, batch, gw):
    t = proj.shape[0]
    heads = gw // HEAD_DIM
    c = RET_CHUNK
    nc = seq // c
    pos = jnp.arange(seq)
    inv = 1.0 / (RET_ROT_BASE ** jnp.linspace(0.0, 1.0, HEAD_DIM // 2, dtype=F32))
    ang = pos.astype(F32)[:, None] * inv[None, :]
    cos, sin = jnp.cos(ang), jnp.sin(ang)
    cos_t = jnp.concatenate([cos, cos], axis=1)
    sin_t = jnp.concatenate([-sin, sin], axis=1)
    log_g = jnp.log(1.0 - 2.0 ** (-5.0 - jnp.arange(heads, dtype=F32)))
    idx = jnp.arange(c, dtype=F32)
    rel = idx[:, None] - idx[None, :]
    inner_decay = jnp.where(rel >= 0, jnp.exp(log_g[:, None, None] * jnp.maximum(rel, 0.0)), 0.0)
    q_decay = jnp.broadcast_to(jnp.exp(log_g[:, None] * (idx + 1.0))[..., None], (heads, c, HEAD_DIM))
    k_decay = jnp.broadcast_to(jnp.exp(log_g[:, None] * (c - 1.0 - idx))[..., None], (heads, c, HEAD_DIM))
    chunk_decay = jnp.broadcast_to(jnp.exp(log_g * c)[:, None, None], (heads, HEAD_DIM, HEAD_DIM))

    def col(n):
        return pl.BlockSpec((c, gw), lambda b, i: (b * nc + i, n))

    rope_spec = pl.BlockSpec((c, HEAD_DIM), lambda b, i: (i, 0))
    const_spec = pl.BlockSpec((heads, c, HEAD_DIM), lambda b, i: (0, 0, 0))
    return pl.pallas_call(
        functools.partial(_retention_body, heads=heads),
        grid=(batch, nc),
        in_specs=[col(0), col(1), col(2), col(3), rope_spec, rope_spec,
                  const_spec, const_spec, const_spec, const_spec],
        out_specs=pl.BlockSpec((c, gw), lambda b, i: (b * nc + i, 0)),
        out_shape=jax.ShapeDtypeStruct((t, gw), BF16),
        scratch_shapes=[pltpu.VMEM((heads, HEAD_DIM, HEAD_DIM), F32)],
        compiler_params=_params("parallel", "arbitrary"),
        name="retention",
    )(proj, proj, proj, proj, cos_t, sin_t, inner_decay, q_decay, k_decay, chunk_decay)


def _stickbreak_body(q_ref, k_ref, v_ref, tri_ref, o_ref, acc_ref, later_ref, *, bk, heads):
    i = pl.program_id(2)
    tq = q_ref.shape[0]
    tri = tri_ref[...]
    scale = HEAD_DIM ** -0.5
    acc_ref[...] = jnp.zeros(acc_ref.shape, F32)
    later_ref[...] = jnp.zeros(later_ref.shape, F32)
    qpos = i * tq + lax.broadcasted_iota(jnp.int32, (tq, bk), 0)
    kcol = lax.broadcasted_iota(jnp.int32, (tq, bk), 1)

    def tile(j, masked):
        start = pl.multiple_of(j * bk, bk)
        if masked:
            past = start + kcol < qpos
        old = [(later_ref[h], acc_ref[h]) for h in range(heads)]
        new = []
        for h in range(heads):
            cols = slice(h * HEAD_DIM, (h + 1) * HEAD_DIM)
            z = _dot_nt(q_ref[:, cols], k_ref[pl.ds(start, bk), cols]) * scale
            stay = jnp.maximum(z, 0.0) + jnp.log(1.0 + jnp.exp(-jnp.abs(z)))
            if masked:
                stay = jnp.where(past, stay, 0.0)
            hi = stay.astype(BF16)
            lo = (stay - hi.astype(F32)).astype(BF16)
            both = _dot(jnp.concatenate([hi, lo], axis=0), tri)
            later, acc = old[h]
            w = jnp.exp(z - (both[:tq, :] + both[tq:, :] + later))
            if masked:
                w = jnp.where(past, w, 0.0)
            new.append((later + jnp.sum(stay, axis=1, keepdims=True),
                        acc + _dot(w.astype(BF16), v_ref[pl.ds(start, bk), cols])))
        for h in range(heads):
            later_ref[h], acc_ref[h] = new[h]

    n_diag = tq // bk
    for d in reversed(range(n_diag)):
        tile(i * n_diag + d, True)

    def body(jj, c):
        tile(i * n_diag - 1 - jj, False)
        return c

    lax.fori_loop(0, i * n_diag, body, 0)
    for h in range(heads):
        o_ref[:, h * HEAD_DIM:(h + 1) * HEAD_DIM] = acc_ref[h].astype(o_ref.dtype)


def _stickbreak(proj, seq, batch, gw, col0):
    t = proj.shape[0]
    tq, bk = 512, 256
    nq = seq // tq
    hb = 2
    wb = hb * HEAD_DIM
    c0 = col0 * HEAD_DIM // wb
    per = gw // wb
    ii = np.arange(bk)
    tri = jnp.asarray(ii[:, None] >= ii[None, :], BF16)
    kv_spec = lambda n: pl.BlockSpec((seq, wb), lambda b, h, i: (b, c0 + n * per + h))
    return pl.pallas_call(
        functools.partial(_stickbreak_body, bk=bk, heads=hb),
        grid=(batch, per, nq),
        in_specs=[pl.BlockSpec((tq, wb), lambda b, h, i: (b * nq + i, c0 + h)),
                  kv_spec(1), kv_spec(2),
                  pl.BlockSpec((bk, bk), lambda b, h, i: (0, 0))],
        out_specs=pl.BlockSpec((tq, wb), lambda b, h, i: (b * nq + i, h)),
        out_shape=jax.ShapeDtypeStruct((t, gw), BF16),
        scratch_shapes=[pltpu.VMEM((hb, tq, HEAD_DIM), F32), pltpu.VMEM((hb, tq, 1), F32)],
        compiler_params=_params("parallel", "parallel", "arbitrary"),
        name="stickbreak",
    )(proj, proj, proj, tri)


def _rglru_body(gate_ref, rec_ref, cw_ref, cb_ref, wa_ref, ba_ref, wx_ref, bx_ref, lam_ref, o_ref,
                xbuf, abuf, ubuf, hprev, *, blocks):
    i = pl.program_id(1)
    ts = rec_ref.shape[0]
    gw = rec_ref.shape[1]

    @pl.when(i == 0)
    def _():
        xbuf[0:SUBLANES, :] = jnp.zeros((SUBLANES, gw), F32)
        hprev[...] = jnp.zeros(hprev.shape, F32)

    x = rec_ref[...].astype(F32)
    xbuf[SUBLANES:SUBLANES + ts, :] = x
    cw = cw_ref[0]
    xr = cb_ref[0] + cw[3:4, :] * x
    for k in range(LRU_CONV - 1):
        off = SUBLANES - (LRU_CONV - 1) + k
        xr = xr + cw[k:k + 1, :] * xbuf[off:off + ts, :]
    xbuf[0:SUBLANES, :] = x[ts - SUBLANES:ts, :]

    log_sig_lam = -_softplus(-lam_ref[0])
    xrb = xr.astype(BF16)
    bw = gw // blocks
    first = (lax.broadcasted_iota(jnp.int32, (ts, bw), 0) == 0) & (i == 0)
    for n in range(blocks):
        cols = slice(n * bw, (n + 1) * bw)
        r = _sigmoid(_dot(xrb[:, cols], wa_ref[0, n]) + ba_ref[0, :, cols])
        gi = _sigmoid(_dot(xrb[:, cols], wx_ref[0, n]) + bx_ref[0, :, cols])
        log_a = LRU_C * r * log_sig_lam[:, cols]
        mult = jnp.where(first, 1.0, jnp.sqrt(jnp.maximum(1.0 - jnp.exp(2.0 * log_a), 0.0)))
        abuf[:, cols] = jnp.exp(log_a)
        ubuf[:, cols] = mult * (gi * xr[:, cols])

    rows = lax.broadcasted_iota(jnp.int32, (SUBLANES, gw), 0)

    def group(gidx, h):
        start = pl.multiple_of(gidx * SUBLANES, SUBLANES)
        a = abuf[pl.ds(start, SUBLANES), :]
        u = ubuf[pl.ds(start, SUBLANES), :]
        for s in (1, 2, 4):
            keep = rows >= s
            a_sh = jnp.where(keep, pltpu.roll(a, s, axis=0), 1.0)
            u_sh = jnp.where(keep, pltpu.roll(u, s, axis=0), 0.0)
            u = u + a * u_sh
            a = a * a_sh
        hs = u + a * h
        ubuf[pl.ds(start, SUBLANES), :] = hs
        return jnp.broadcast_to(hs[SUBLANES - 1:SUBLANES, :], (SUBLANES, gw))

    hprev[...] = lax.fori_loop(0, ts // SUBLANES, group, hprev[...])
    o_ref[...] = (ubuf[...] * _gelu_tanh(gate_ref[...].astype(F32))).astype(o_ref.dtype)


def _rglru(proj, cw, cb3, wa, ba3, wx, bx3, lam3, layer, seq, batch, gw, col0):
    t = proj.shape[0]
    blocks = gw // HEAD_DIM
    ts = 256
    ns = seq // ts
    cpb = col0 // blocks
    vec_spec = pl.BlockSpec((1, 1, gw), lambda b, i: (layer, 0, 0))
    w_spec = pl.BlockSpec((1, blocks, HEAD_DIM, HEAD_DIM), lambda b, i: (layer, 0, 0, 0))
    return pl.pallas_call(
        functools.partial(_rglru_body, blocks=blocks),
        grid=(batch, ns),
        in_specs=[pl.BlockSpec((ts, gw), lambda b, i: (b * ns + i, cpb)),
                  pl.BlockSpec((ts, gw), lambda b, i: (b * ns + i, cpb + 1)),
                  pl.BlockSpec((1, LRU_CONV, gw), lambda b, i: (layer, 0, 0)),
                  vec_spec, w_spec, vec_spec, w_spec, vec_spec, vec_spec],
        out_specs=pl.BlockSpec((ts, gw), lambda b, i: (b * ns + i, 0)),
        out_shape=jax.ShapeDtypeStruct((t, gw), BF16),
        scratch_shapes=[pltpu.VMEM((SUBLANES + ts, gw), F32), pltpu.VMEM((ts, gw), F32),
                        pltpu.VMEM((ts, gw), F32), pltpu.VMEM((SUBLANES, gw), F32)],
        compiler_params=_params("parallel", "arbitrary"),
        name="rglru",
    )(proj, proj, cw, cb3, wa, ba3, wx, bx3, lam3)


def _rope_tables(pos):
    half = ROPE_DIMS // 2
    inv = ROPE_THETA ** (-jnp.arange(half, dtype=F32) / half)
    ang = pos.astype(F32)[:, None] * inv[None, :]
    cos, sin = jnp.cos(ang), jnp.sin(ang)
    n = pos.shape[0]
    zeros = jnp.zeros((n, half), F32)
    rest = HEAD_DIM - 2 * half
    c = jnp.concatenate([cos, cos, jnp.ones((n, rest), F32)], axis=1)
    s_lo = jnp.concatenate([-sin, zeros, jnp.zeros((n, rest), F32)], axis=1)
    s_hi = jnp.concatenate([zeros, sin, jnp.zeros((n, rest), F32)], axis=1)
    return c, s_lo, s_hi


def _partial_rope(x, c, s_lo, s_hi):
    half = ROPE_DIMS // 2
    return x * c + pltpu.roll(x, HEAD_DIM - half, axis=1) * s_lo + pltpu.roll(x, half, axis=1) * s_hi


def _nsa_prep_body(q_ref, ks_ref, kw_ref, qw_ref, kwt_ref, c_ref, lo_ref, hi_ref, qo_ref, ko_ref,
                   *, heads, kv_heads):
    c, s_lo, s_hi = c_ref[...], lo_ref[...], hi_ref[...]

    def prep(x, w):
        return _partial_rope(_rms_rows(x) * w, c, s_lo, s_hi)

    qw = qw_ref[0]
    for h in range(heads):
        cols = slice(h * HEAD_DIM, (h + 1) * HEAD_DIM)
        qo_ref[:, cols] = (prep(q_ref[:, cols].astype(F32), qw) * (HEAD_DIM ** -0.5)).astype(qo_ref.dtype)
    for n, src in enumerate((ks_ref, kw_ref)):
        w = kwt_ref[0, n + 1:n + 2, :]
        for g in range(kv_heads):
            cols = slice(g * HEAD_DIM, (g + 1) * HEAD_DIM)
            dst = slice((n * kv_heads + g) * HEAD_DIM, (n * kv_heads + g + 1) * HEAD_DIM)
            ko_ref[:, dst] = prep(src[:, cols].astype(F32), w).astype(ko_ref.dtype)


def _nsa_prep(proj, qw3, kw3, tables, layer, seq, batch, gw, colq, colkv):
    t = proj.shape[0]
    heads = gw // HEAD_DIM
    kvw = NSA_KV_HEADS * HEAD_DIM
    ts = 256
    ns = seq // ts
    rope_spec = pl.BlockSpec((ts, HEAD_DIM), lambda b, i: (i, 0))
    cq = colq * HEAD_DIM // gw
    ck = colkv * HEAD_DIM // kvw
    return pl.pallas_call(
        functools.partial(_nsa_prep_body, heads=heads, kv_heads=NSA_KV_HEADS),
        grid=(batch, ns),
        in_specs=[pl.BlockSpec((ts, gw), lambda b, i: (b * ns + i, cq)),
                  pl.BlockSpec((ts, kvw), lambda b, i: (b * ns + i, ck + 2)),
                  pl.BlockSpec((ts, kvw), lambda b, i: (b * ns + i, ck + 4)),
                  pl.BlockSpec((1, 1, HEAD_DIM), lambda b, i: (layer, 0, 0)),
                  pl.BlockSpec((1, 3, HEAD_DIM), lambda b, i: (layer, 0, 0)),
                  rope_spec, rope_spec, rope_spec],
        out_specs=[pl.BlockSpec((ts, gw), lambda b, i: (b * ns + i, 0)),
                   pl.BlockSpec((ts, 2 * kvw), lambda b, i: (b * ns + i, 0))],
        out_shape=[jax.ShapeDtypeStruct((t, gw), BF16), jax.ShapeDtypeStruct((t, 2 * kvw), BF16)],
        compiler_params=_params("parallel", "parallel"),
        name="nsa_prep",
    )(proj, proj, proj, qw3, kw3, *tables)


def _compress_body(seg_ref, pos_ref, w1_ref, w2_ref, kw_ref, c_ref, lo_ref, hi_ref, o_ref):
    seg = seg_ref[0, 0]
    ns = seg.shape[0]
    first = _dot((seg + pos_ref[0, 0, 0:1, :]).astype(BF16), w1_ref[0, 0, 0])
    second = _dot((seg + pos_ref[0, 0, 1:2, :]).astype(BF16), w1_ref[0, 0, 1])
    hid = _gelu_tanh(first + pltpu.roll(second, ns - 1, axis=0))
    out = _dot(hid.astype(BF16), w2_ref[0, 0])

    @pl.when(pl.program_id(1) < NSA_KV_HEADS)
    def _():
        o_ref[0, 0] = _partial_rope(_rms_rows(out) * kw_ref[0, 0:1, :], c_ref[...], lo_ref[...], hi_ref[...])

    @pl.when(pl.program_id(1) >= NSA_KV_HEADS)
    def _():
        o_ref[0, 0] = out


def _compress(segs, pos4, w1, w2, kw3, tables, layer):
    batch, n4, ns, width = segs.shape
    hidden = w1.shape[-1]
    g = NSA_KV_HEADS
    full = pl.BlockSpec((ns, HEAD_DIM), lambda b, c: (0, 0))
    return pl.pallas_call(
        _compress_body,
        grid=(batch, n4),
        in_specs=[pl.BlockSpec((1, 1, ns, width), lambda b, c: (b, c, 0, 0)),
                  pl.BlockSpec((1, 1, 2, width), lambda b, c: (layer, c // g, 0, 0)),
                  pl.BlockSpec((1, 1, 2, width, hidden), lambda b, c: (layer, c // g, 0, 0, 0)),
                  pl.BlockSpec((1, 1, hidden, HEAD_DIM), lambda b, c: (layer, c // g, 0, 0)),
                  pl.BlockSpec((1, 3, HEAD_DIM), lambda b, c: (layer, 0, 0)),
                  full, full, full],
        out_specs=pl.BlockSpec((1, 1, ns, HEAD_DIM), lambda b, c: (b, c, 0, 0)),
        out_shape=jax.ShapeDtypeStruct((batch, n4, ns, HEAD_DIM), F32),
        compiler_params=_params("parallel", "parallel"),
        name="nsa_compress",
    )(segs, pos4, w1, w2, kw3, *tables)


def _cmp_select_body(q_ref, kc_ref, vc_ref, ov_ref, o_ref, sel_ref, score_t, *, rep, n_cmp, n_slc, top_k):
    i = pl.program_id(2)
    tq = q_ref.shape[0]
    ncp = kc_ref.shape[2]
    nsp = ov_ref.shape[1]
    kc = kc_ref[0, 0].astype(BF16)
    vc = vc_ref[0, 0].astype(BF16)
    tpos = i * tq + lax.broadcasted_iota(jnp.int32, (tq, ncp), 0)
    blk_n = lax.broadcasted_iota(jnp.int32, (tq, ncp), 1)
    visible = (blk_n * CMP_STRIDE + (CMP_LEN - 1) <= tpos) & (blk_n < n_cmp)
    p_sum = jnp.zeros((tq, ncp), F32)
    for r in range(rep):
        cols = slice(r * HEAD_DIM, (r + 1) * HEAD_DIM)
        s = jnp.where(visible, _dot_nt(q_ref[:, cols], kc), -jnp.inf)
        m = jnp.max(s, axis=-1, keepdims=True)
        m = jnp.where(m > -jnp.inf, m, 0.0)
        e = jnp.where(visible, jnp.exp(s - m), 0.0)
        p = e / jnp.maximum(jnp.sum(e, axis=-1, keepdims=True), 1e-30)
        o_ref[:, cols] = _dot(p.astype(BF16), vc)
        p_sum = p_sum + p
    p_slc = _dot_split(p_sum, ov_ref[...])

    tpos_s = i * tq + lax.broadcasted_iota(jnp.int32, (tq, nsp), 0)
    blk_s = lax.broadcasted_iota(jnp.int32, (tq, nsp), 1)
    cur = tpos_s // SLC_LEN
    forced = (blk_s == 0) | (blk_s == cur) | (blk_s == cur - 1)
    valid = blk_s <= cur
    score = jnp.where(forced, FORCED_SCORE, jnp.where(valid, p_slc, NEG_BIG))
    score_t[...] = score.T
    n_rows = _round_up(n_slc, SUBLANES)
    mine = score_t[0:n_rows, :]
    my_blk = lax.broadcasted_iota(jnp.int32, (n_rows, tq), 0)

    def count(other, rank):
        row = score_t[pl.ds(other, 1), :]
        strictly = jnp.where(row > mine, 1.0, 0.0)
        or_equal = jnp.where(row >= mine, 1.0, 0.0)
        return rank + jnp.where(my_blk > other, or_equal, strictly)

    rank = lax.fori_loop(0, n_slc, count, jnp.zeros((n_rows, tq), F32), unroll=4)
    score_t[0:n_rows, :] = jnp.where(rank < top_k, 1.0, 0.0)
    chosen = score_t[...].T
    sel_ref[0] = jnp.where(valid, chosen, 0.0).astype(sel_ref.dtype)


def _cmp_select(qn, cmp_kv, overlap, seq, batch, gw):
    t = qn.shape[0]
    g = NSA_KV_HEADS
    rep = gw // HEAD_DIM // g
    ncp = cmp_kv.shape[2]
    nsp = overlap.shape[1]
    n_slc = seq // SLC_LEN
    tq = 256
    nq = seq // tq
    body = functools.partial(_cmp_select_body, rep=rep, n_cmp=(seq - CMP_LEN) // CMP_STRIDE + 1,
                             n_slc=n_slc, top_k=min(SLC_TOPK, n_slc))
    return pl.pallas_call(
        body,
        grid=(batch, g, nq),
        in_specs=[pl.BlockSpec((tq, rep * HEAD_DIM), lambda b, gi, i: (b * nq + i, gi)),
                  pl.BlockSpec((1, 1, ncp, HEAD_DIM), lambda b, gi, i: (b, gi, 0, 0)),
                  pl.BlockSpec((1, 1, ncp, HEAD_DIM), lambda b, gi, i: (b, g + gi, 0, 0)),
                  pl.BlockSpec((ncp, nsp), lambda b, gi, i: (0, 0))],
        out_specs=[pl.BlockSpec((tq, rep * HEAD_DIM), lambda b, gi, i: (b * nq + i, gi)),
                   pl.BlockSpec((1, tq, nsp), lambda b, gi, i: (gi, b * nq + i, 0))],
        out_shape=[jax.ShapeDtypeStruct((t, gw), F32), jax.ShapeDtypeStruct((g, t, nsp), BF16)],
        scratch_shapes=[pltpu.VMEM((nsp, tq), F32)],
        compiler_params=_params("parallel", "parallel", "parallel"),
        name="nsa_cmp_select",
    )(qn, cmp_kv, cmp_kv, overlap)


def _attend_body(q_ref, ks_ref, vs_ref, kw_ref, vw_ref, sel_ref, ex_ref, oc_ref, gl_ref, o_ref,
                 s_ref, m_ref, l_ref, acc_ref, *, rep, groups, blk, span):
    i = pl.program_id(1)
    tq = q_ref.shape[0]
    rows = rep * tq
    folds = blk // LANES
    n_blocks = ((i + 1) * tq + blk - 1) // blk

    def stacked_q(g):
        return jnp.concatenate(
            [q_ref[:, (g * rep + r) * HEAD_DIM:(g * rep + r + 1) * HEAD_DIM] for r in range(rep)], axis=0)

    m_ref[...] = jnp.full(m_ref.shape, NEG_BIG, F32)
    l_ref[...] = jnp.zeros(l_ref.shape, F32)
    acc_ref[...] = jnp.zeros(acc_ref.shape, F32)
    tpos = i * tq + lax.broadcasted_iota(jnp.int32, (tq, blk), 0)
    kcol = lax.broadcasted_iota(jnp.int32, (tq, blk), 1)

    def scores(j, carry):
        start = pl.multiple_of(j * blk, blk)
        causal = start + kcol <= tpos
        for g in range(groups):
            cols = slice(g * HEAD_DIM, (g + 1) * HEAD_DIM)
            picked = _dot(sel_ref[g], ex_ref[j])
            bias = jnp.where((picked > 0.5) & causal, 0.0, NEG_BIG)
            s = _dot_nt(stacked_q(g), ks_ref[pl.ds(start, blk), cols]) + jnp.concatenate([bias] * rep, axis=0)
            s_ref[g, j] = s
            m = m_ref[g]
            for c in range(folds):
                m = jnp.maximum(m, s[:, c * LANES:(c + 1) * LANES])
            m_ref[g] = m
        return carry

    lax.fori_loop(0, n_blocks, scores, 0)

    for g in range(groups):
        m_ref[g] = jnp.broadcast_to(jnp.max(m_ref[g], axis=-1, keepdims=True), (rows, LANES))

    def weights(j, carry):
        start = pl.multiple_of(j * blk, blk)
        for g in range(groups):
            cols = slice(g * HEAD_DIM, (g + 1) * HEAD_DIM)
            p = jnp.exp(s_ref[g, j] - jnp.concatenate([m_ref[g]] * folds, axis=1))
            l = l_ref[g]
            for c in range(folds):
                l = l + p[:, c * LANES:(c + 1) * LANES]
            l_ref[g] = l
            acc_ref[g] += _dot(p.astype(BF16), vs_ref[pl.ds(start, blk), cols])
        return carry

    lax.fori_loop(0, n_blocks, weights, 0)

    wstart = pl.multiple_of(jnp.maximum(i * tq + tq - span, 0), tq)
    wpos = i * tq + (lax.broadcasted_iota(jnp.int32, (rep * tq, span), 0) & (tq - 1))
    dist = wpos - (wstart + lax.broadcasted_iota(jnp.int32, (rep * tq, span), 1))
    wbias = jnp.where((dist >= 0) & (dist < WINDOW), 0.0, NEG_BIG)
    gates = _sigmoid(gl_ref[...])
    lane = lax.broadcasted_iota(jnp.int32, gates.shape, 1)

    def gate(head, branch):
        return jnp.sum(jnp.where(lane == 3 * head + branch, gates, 0.0), axis=-1, keepdims=True)

    for g in range(groups):
        cols = slice(g * HEAD_DIM, (g + 1) * HEAD_DIM)
        s = _dot_nt(stacked_q(g), kw_ref[pl.ds(wstart, span), cols]) + wbias
        e = jnp.exp(s - jnp.max(s, axis=-1, keepdims=True))
        o_win = _dot(e.astype(BF16), vw_ref[pl.ds(wstart, span), cols]) / jnp.sum(e, axis=-1, keepdims=True)
        o_slc = acc_ref[g] / jnp.sum(l_ref[g], axis=-1, keepdims=True)
        for r in range(rep):
            head = g * rep + r
            hc = slice(head * HEAD_DIM, (head + 1) * HEAD_DIM)
            rows = slice(r * tq, (r + 1) * tq)
            mix = gate(head, 0) * oc_ref[:, hc] + gate(head, 1) * o_slc[rows, :] + gate(head, 2) * o_win[rows, :]
            o_ref[:, hc] = mix.astype(o_ref.dtype)


def _attend(qn, kk, proj, sel, expand, o_cmp, gl, seq, batch, gw, colkv):
    t = qn.shape[0]
    g = NSA_KV_HEADS
    kvw = g * HEAD_DIM
    rep = gw // HEAD_DIM // g
    nsp = sel.shape[2]
    tq = 128
    blk = expand.shape[2]
    nq = seq // tq
    span = WINDOW + tq
    ck = colkv * HEAD_DIM // kvw
    row_spec = pl.BlockSpec((tq, gw), lambda b, i: (b * nq + i, 0))
    return pl.pallas_call(
        functools.partial(_attend_body, rep=rep, groups=g, blk=blk, span=span),
        grid=(batch, nq),
        in_specs=[row_spec,
                  pl.BlockSpec((seq, kvw), lambda b, i: (b, 0)),
                  pl.BlockSpec((seq, kvw), lambda b, i: (b, ck + 3)),
                  pl.BlockSpec((seq, kvw), lambda b, i: (b, 1)),
                  pl.BlockSpec((seq, kvw), lambda b, i: (b, ck + 5)),
                  pl.BlockSpec((g, tq, nsp), lambda b, i: (0, b * nq + i, 0)),
                  pl.BlockSpec((seq // blk, nsp, blk), lambda b, i: (0, 0, 0)),
                  row_spec,
                  pl.BlockSpec((tq, LANES), lambda b, i: (b * nq + i, 0))],
        out_specs=row_spec,
        out_shape=jax.ShapeDtypeStruct((t, gw), BF16),
        scratch_shapes=[pltpu.VMEM((g, seq // blk, rep * tq, blk), F32),
                        pltpu.VMEM((g, rep * tq, LANES), F32), pltpu.VMEM((g, rep * tq, LANES), F32),
                        pltpu.VMEM((g, rep * tq, HEAD_DIM), F32)],
        compiler_params=_params("parallel", "parallel"),
        name="nsa_attend",
    )(qn, kk, proj, kk, proj, sel, expand, o_cmp, gl)


def _nsa(proj, gl, qw3, kw3, pos4, w1, w2, layer, seq, batch, gw, colq, colkv):
    g = NSA_KV_HEADS
    kvw = g * HEAD_DIM
    tables = _rope_tables(jnp.arange(seq))
    qn, kk = _nsa_prep(proj, qw3, kw3, tables, layer, seq, batch, gw, colq, colkv)

    ns = seq // CMP_STRIDE
    n_cmp = (seq - CMP_LEN) // CMP_STRIDE + 1
    c0 = colkv * HEAD_DIM
    segs = proj[:, c0:c0 + 2 * kvw].reshape(batch, ns, CMP_STRIDE, 2 * g, HEAD_DIM)
    segs = segs.transpose(0, 3, 1, 2, 4).reshape(batch, 2 * g, ns, CMP_STRIDE * HEAD_DIM)
    cmp_end = jnp.arange(ns) * CMP_STRIDE + CMP_LEN - 1
    cmp_kv = _compress(segs, pos4, w1, w2, kw3, _rope_tables(cmp_end), layer)

    n_slc = seq // SLC_LEN
    nsp = _round_up(n_slc, LANES)
    ci = np.arange(ns)[:, None] * CMP_STRIDE
    sj = np.arange(nsp)[None, :] * SLC_LEN
    ov = np.maximum(0, np.minimum(ci + CMP_LEN, sj + SLC_LEN) - np.maximum(ci, sj)) / CMP_STRIDE
    ov = np.where((np.arange(ns)[:, None] < n_cmp) & (np.arange(nsp)[None, :] < n_slc), ov, 0.0)
    o_cmp, sel = _cmp_select(qn, cmp_kv, jnp.asarray(ov, BF16), seq, batch, gw)

    blk = 512
    key_blk = (np.arange(seq) // SLC_LEN).reshape(seq // blk, 1, blk)
    expand = jnp.asarray(np.arange(nsp)[None, :, None] == key_blk, BF16)
    return _attend(qn, kk, proj, sel, expand, o_cmp, gl, seq, batch, gw, colkv)


def kernel(x, attn_norm_w, w_in, lru_conv_w, lru_conv_b, lru_w_a, lru_b_a, lru_w_x, lru_b_x, lru_lambda,
           nsa_q_norm_w, nsa_k_norm_w, nsa_cmp_pos, nsa_cmp_w1, nsa_cmp_w2, w_out, mlp_norm_w,
           w_gate, w_up, mlp_conv_w, mlp_conv_b, w_down):
    batch, seq, d = x.shape
    depth = w_in.shape[0]
    gw = d // N_MIXERS
    gh = gw // HEAD_DIM
    kvw = NSA_KV_HEADS * HEAD_DIM
    n_main = 10 * gw + 6 * kvw
    n_gate = w_in.shape[2] - n_main
    t = batch * seq
    assert seq % 512 == 0 and seq >= WINDOW + 128 and gw % (NSA_KV_HEADS * HEAD_DIM) == 0
    assert n_main % 512 == 0 and n_gate <= LANES and d % 1024 == 0 and w_gate.shape[2] % LANES == 0

    w_main = w_in[:, :, :n_main].astype(BF16)
    w_gl = jnp.pad(w_in[:, :, n_main:], ((0, 0), (0, 0), (0, LANES - n_gate))).astype(BF16)
    w_out_b = w_out.astype(BF16)
    w_gate_b = w_gate.astype(BF16)
    w_up_b = w_up.astype(BF16)
    w_down_b = w_down.astype(BF16)
    mlp_cb3 = mlp_conv_b[:, None, :]
    attn_w3 = attn_norm_w[:, None, :]
    mlp_w3 = mlp_norm_w[:, None, :]
    lru_cb3, lru_ba3, lru_bx3, lru_lam3 = (v[:, None, :] for v in (lru_conv_b, lru_b_a, lru_b_x, lru_lambda))
    lru_wa_b = lru_w_a.astype(BF16)
    lru_wx_b = lru_w_x.astype(BF16)
    qw3 = nsa_q_norm_w[:, None, :]
    seg_w = CMP_STRIDE * HEAD_DIM
    pos4 = nsa_cmp_pos.reshape(depth, 2, CMP_LEN // CMP_STRIDE, seg_w)
    cmp_w1 = nsa_cmp_w1.reshape(depth, 2, CMP_LEN // CMP_STRIDE, seg_w, -1).astype(BF16)
    cmp_w2 = nsa_cmp_w2.astype(BF16)

    xf = x.reshape(t, d)
    for layer in range(depth):
        hn = _rmsnorm(xf, attn_w3, layer)
        proj = _matmul(hn, w_main, layer, 1024, 512, BF16, "inproj")
        gl = _matmul(hn, w_gl, layer, 1024, LANES, F32, "inproj_gates")
        y_ret = _retention(proj, seq, batch, gw)
        y_sb = _stickbreak(proj, seq, batch, gw, 4 * gh)
        y_lru = _rglru(proj, lru_conv_w, lru_cb3, lru_wa_b, lru_ba3, lru_wx_b, lru_bx3, lru_lam3,
                       layer, seq, batch, gw, 7 * gh)
        y_nsa = _nsa(proj, gl, qw3, nsa_k_norm_w, pos4, cmp_w1, cmp_w2, layer, seq, batch, gw,
                     9 * gh, 10 * gh)
        xf = _outproj((y_ret, y_sb, y_lru, y_nsa), w_out_b, xf, layer, 512, 512)
        hn = _rmsnorm(xf, mlp_w3, layer)
        hid = _gateup(hn, w_gate_b, w_up_b, mlp_conv_w, mlp_cb3, layer, seq, 512, 512)
        xf = _down(hid, w_down_b, xf, layer, 256, 1024)
    return xf.reshape(batch, seq, d)
```

```python
import functools
import math

import numpy as np
import jax
import jax.numpy as jnp
from jax import lax
from jax.experimental import pallas as pl
from jax.experimental.pallas import tpu as pltpu

F32 = jnp.float32
BF16 = jnp.bfloat16

HEAD_DIM = 128
N_MIXERS = 4
RET_CHUNK = 128
RET_ROT_BASE = 10000.0
LRU_CONV = 4
LRU_C = 8.0
NSA_KV_HEADS = 2
CMP_LEN = 32
CMP_STRIDE = 16
SLC_LEN = 64
SLC_TOPK = 16
WINDOW = 512
ROPE_THETA = 500000.0
ROPE_DIMS = HEAD_DIM // 4
MLP_CONV = 3
NORM_EPS = 1e-6

LANES = 128
SUBLANES = 8
VMEM_LIMIT_BYTES = 56 * 1024 * 1024
NEG_BIG = -1e30
FORCED_SCORE = 1e30


def _params(*sem, flags=None):
    return pltpu.CompilerParams(dimension_semantics=sem, vmem_limit_bytes=VMEM_LIMIT_BYTES, flags=flags)


def _round_up(x, m):
    return (x + m - 1) // m * m


def _dot(a, b):
    return jnp.dot(a, b, preferred_element_type=F32)


def _dot_nt(a, b):
    return lax.dot_general(a, b, (((1,), (1,)), ((), ())), preferred_element_type=F32)


def _dot_split(x, m):
    hi = x.astype(BF16)
    lo = (x - hi.astype(F32)).astype(BF16)
    return _dot(hi, m) + _dot(lo, m)


def _sigmoid(x):
    return 1.0 / (1.0 + jnp.exp(-x))


def _softplus(x):
    return jnp.maximum(x, 0.0) + jnp.log1p(jnp.exp(-jnp.abs(x)))


def _gelu_tanh(x):
    return x * (0.5 * (1.0 + jnp.tanh(math.sqrt(2.0 / math.pi) * (x + 0.044715 * (x * x * x)))))


def _rms_rows(x):
    return x * lax.rsqrt(jnp.mean(x * x, axis=-1, keepdims=True) + NORM_EPS)


def _rmsnorm_body(x_ref, w_ref, o_ref):
    o_ref[...] = (_rms_rows(x_ref[...]) * w_ref[0]).astype(o_ref.dtype)


def _rmsnorm(x, w3, layer):
    t, d = x.shape
    tr = 256
    return pl.pallas_call(
        _rmsnorm_body,
        grid=(t // tr,),
        in_specs=[pl.BlockSpec((tr, d), lambda i: (i, 0)),
                  pl.BlockSpec((1, 1, d), lambda i: (layer, 0, 0))],
        out_specs=pl.BlockSpec((tr, d), lambda i: (i, 0)),
        out_shape=jax.ShapeDtypeStruct((t, d), BF16),
        compiler_params=_params("parallel"),
        name="rmsnorm",
    )(x, w3)


def _matmul_body(a_ref, w_ref, o_ref):
    o_ref[...] = _dot(a_ref[...], w_ref[0]).astype(o_ref.dtype)


def _matmul(a, w, layer, col0, n, tm, tn, out_dtype, name):
    t, k = a.shape
    return pl.pallas_call(
        _matmul_body,
        grid=(t // tm, n // tn),
        in_specs=[pl.BlockSpec((tm, k), lambda i, j: (i, 0)),
                  pl.BlockSpec((1, k, tn), lambda i, j: (layer, 0, col0 + j))],
        out_specs=pl.BlockSpec((tm, tn), lambda i, j: (i, j)),
        out_shape=jax.ShapeDtypeStruct((t, n), out_dtype),
        compiler_params=_params("parallel", "parallel"),
        name=name,
    )(a, w)


def _outproj_body(y0_ref, y1_ref, y2_ref, y3_ref, w_ref, x_ref, o_ref):
    gw = y0_ref.shape[1]
    acc = x_ref[...]
    for n, y_ref in enumerate((y0_ref, y1_ref, y2_ref, y3_ref)):
        acc = acc + _dot(y_ref[...], w_ref[0, n * gw:(n + 1) * gw, :])
    o_ref[...] = acc


def _outproj(ys, w, x, layer, tm, tn):
    t, d = x.shape
    gw = ys[0].shape[1]
    y_spec = pl.BlockSpec((tm, gw), lambda i, j: (i, 0))
    return pl.pallas_call(
        _outproj_body,
        grid=(t // tm, d // tn),
        in_specs=[y_spec, y_spec, y_spec, y_spec,
                  pl.BlockSpec((1, N_MIXERS * gw, tn), lambda i, j: (layer, 0, j)),
                  pl.BlockSpec((tm, tn), lambda i, j: (i, j))],
        out_specs=pl.BlockSpec((tm, tn), lambda i, j: (i, j)),
        out_shape=jax.ShapeDtypeStruct((t, d), F32),
        compiler_params=_params("parallel", "parallel"),
        name="outproj",
    )(*ys, w, x)


def _gateup_body(a_ref, wg_ref, wu_ref, cw_ref, cb_ref, o_ref, gbuf, *, tiles_per_seq, sub):
    i = pl.program_id(1)
    tm = a_ref.shape[0]
    tn = o_ref.shape[1]

    @pl.when(i % tiles_per_seq == 0)
    def _():
        gbuf[0:SUBLANES, :] = jnp.zeros((SUBLANES, tn), F32)

    for c in range(tn // sub):
        cols = slice(c * sub, (c + 1) * sub)
        g = _dot(a_ref[...], wg_ref[0, :, cols])
        u = _dot(a_ref[...], wu_ref[0, :, cols])
        gbuf[SUBLANES:SUBLANES + tm, cols] = g
        cw = cw_ref[0, :, cols]
        gt = (cb_ref[0, :, cols] + cw[2:3, :] * g
              + cw[1:2, :] * gbuf[SUBLANES - 1:SUBLANES - 1 + tm, cols]
              + cw[0:1, :] * gbuf[SUBLANES - 2:SUBLANES - 2 + tm, cols])
        o_ref[:, cols] = (gt * _sigmoid(gt) * u).astype(o_ref.dtype)
        gbuf[0:SUBLANES, cols] = g[tm - SUBLANES:tm, :]


def _gateup(a, wg, wu, cw, cb3, layer, seq, tm, tn):
    t, k = a.shape
    f = wg.shape[2]
    w_spec = pl.BlockSpec((1, k, tn), lambda j, i: (layer, 0, j))
    return pl.pallas_call(
        functools.partial(_gateup_body, tiles_per_seq=seq // tm, sub=2 * LANES),
        grid=(pl.cdiv(f, tn), t // tm),
        in_specs=[pl.BlockSpec((tm, k), lambda j, i: (i, 0)), w_spec, w_spec,
                  pl.BlockSpec((1, MLP_CONV, tn), lambda j, i: (layer, 0, j)),
                  pl.BlockSpec((1, 1, tn), lambda j, i: (layer, 0, j))],
        out_specs=pl.BlockSpec((tm, tn), lambda j, i: (i, j)),
        out_shape=jax.ShapeDtypeStruct((t, f), BF16),
        scratch_shapes=[pltpu.VMEM((SUBLANES + tm, tn), F32)],
        compiler_params=_params("parallel", "arbitrary"),
        name="mlp_gateup",
    )(a, wg, wu, cw, cb3)


def _down_body(h_ref, w_ref, x_ref, o_ref):
    o_ref[...] = x_ref[...] + _dot(h_ref[...], w_ref[0])


def _down(h, w, x, layer, tm, tn):
    t, f = h.shape
    d = x.shape[1]
    return pl.pallas_call(
        _down_body,
        grid=(d // tn, t // tm),
        in_specs=[pl.BlockSpec((tm, f), lambda j, i: (i, 0)),
                  pl.BlockSpec((1, f, tn), lambda j, i: (layer, 0, j), pipeline_mode=pl.Buffered(1)),
                  pl.BlockSpec((tm, tn), lambda j, i: (i, j))],
        out_specs=pl.BlockSpec((tm, tn), lambda j, i: (i, j)),
        out_shape=jax.ShapeDtypeStruct((t, d), F32),
        compiler_params=_params("parallel", "parallel"),
        name="mlp_down",
    )(h, w, x)


def _retention_body(q_ref, k_ref, v_ref, g_ref, cos_ref, sin_ref, inner_ref, qd_ref, kd_ref, cd_ref,
                    o_ref, state, *, heads):
    @pl.when(pl.program_id(1) == 0)
    def _():
        state[...] = jnp.zeros(state.shape, F32)

    cos = cos_ref[...]
    sin = sin_ref[...]

    def rot(x):
        return x * cos + pltpu.roll(x, HEAD_DIM // 2, axis=1) * sin

    for h in range(heads):
        cols = slice(h * HEAD_DIM, (h + 1) * HEAD_DIM)
        q = rot(q_ref[:, cols].astype(F32))
        k = rot(k_ref[:, cols].astype(F32)) * (HEAD_DIM ** -0.5)
        vb = v_ref[:, cols]
        qb = q.astype(BF16)
        inner = _dot_nt(qb, k.astype(BF16)) * inner_ref[h]
        st = state[h]
        o = _dot(inner.astype(BF16), vb) + _dot(qb, st.astype(BF16)) * qd_ref[h]
        kd = (k * kd_ref[h]).T.astype(BF16)
        state[h] = st * cd_ref[h] + _dot(kd, vb)
        g = g_ref[:, cols].astype(F32)
        o_ref[:, cols] = (_rms_rows(o) * (g * _sigmoid(g))).astype(o_ref.dtype)


def _retention(proj, seq, batch, gw):
    t = proj.shape[0]
    heads = gw // HEAD_DIM
    c = RET_CHUNK
    nc = seq // c
    pos = jnp.arange(seq)
    inv = 1.0 / (RET_ROT_BASE ** jnp.linspace(0.0, 1.0, HEAD_DIM // 2, dtype=F32))
    ang = pos.astype(F32)[:, None] * inv[None, :]
    cos, sin = jnp.cos(ang), jnp.sin(ang)
    cos_t = jnp.concatenate([cos, cos], axis=1)
    sin_t = jnp.concatenate([-sin, sin], axis=1)
    log_g = jnp.log(1.0 - 2.0 ** (-5.0 - jnp.arange(heads, dtype=F32)))
    idx = jnp.arange(c, dtype=F32)
    rel = idx[:, None] - idx[None, :]
    inner_decay = jnp.where(rel >= 0, jnp.exp(log_g[:, None, None] * jnp.maximum(rel, 0.0)), 0.0)
    q_decay = jnp.broadcast_to(jnp.exp(log_g[:, None] * (idx + 1.0))[..., None], (heads, c, HEAD_DIM))
    k_decay = jnp.broadcast_to(jnp.exp(log_g[:, None] * (c - 1.0 - idx))[..., None], (heads, c, HEAD_DIM))
    chunk_decay = jnp.broadcast_to(jnp.exp(log_g * c)[:, None, None], (heads, HEAD_DIM, HEAD_DIM))

    def col(n):
        return pl.BlockSpec((c, gw), lambda b, i: (b * nc + i, n))

    rope_spec = pl.BlockSpec((c, HEAD_DIM), lambda b, i: (i, 0))
    const_spec = pl.BlockSpec((heads, c, HEAD_DIM), lambda b, i: (0, 0, 0))
    return pl.pallas_call(
        functools.partial(_retention_body, heads=heads),
        grid=(batch, nc),
        in_specs=[col(0), col(1), col(2), col(3), rope_spec, rope_spec,
                  const_spec, const_spec, const_spec, const_spec],
        out_specs=pl.BlockSpec((c, gw), lambda b, i: (b * nc + i, 0)),
        out_shape=jax.ShapeDtypeStruct((t, gw), BF16),
        scratch_shapes=[pltpu.VMEM((heads, HEAD_DIM, HEAD_DIM), F32)],
        compiler_params=_params("parallel", "arbitrary"),
        name="retention",
    )(proj, proj, proj, proj, cos_t, sin_t, inner_decay, q_decay, k_decay, chunk_decay)


def _stickbreak_body(q_ref, k_ref, v_ref, tri_ref, o_ref, acc_ref, later_ref, *, bk, heads):
    i = pl.program_id(2)
    tq = q_ref.shape[0]
    tri = tri_ref[...]
    log2e = 1.0 / math.log(2.0)
    acc_ref[...] = jnp.zeros(acc_ref.shape, F32)
    later_ref[...] = jnp.zeros(later_ref.shape, F32)
    sign_bit = jnp.uint32(0x80000000)

    def tile(j, row0):
        masked = row0 is not None
        r0 = row0 if masked else 0
        nr = tq - r0
        start = pl.multiple_of(j * bk, bk)
        if masked:
            qpos = i * tq + r0 + lax.broadcasted_iota(jnp.int32, (nr, bk), 0)
            past = start + lax.broadcasted_iota(jnp.int32, (nr, bk), 1) < qpos
        for h in range(heads):
            cols = slice(h * HEAD_DIM, (h + 1) * HEAD_DIM)
            y = _dot_nt(q_ref[r0:, cols], k_ref[pl.ds(start, bk), cols]) * (HEAD_DIM ** -0.5 * log2e)
            minus_abs = pltpu.bitcast(pltpu.bitcast(y, jnp.uint32) | sign_bit, F32)
            stay = jnp.maximum(y, 0.0) + jnp.log(1.0 + jnp.exp2(minus_abs)) * log2e
            if masked:
                stay = jnp.where(past, stay, 0.0)
            hi = stay.astype(BF16)
            both = _dot(hi, tri)
            later = later_ref[h, r0:, :]
            w = jnp.exp2(y - (both + later))
            if masked:
                w = jnp.where(past, w, 0.0)
            acc_ref[h, r0:, :] += _dot(w.astype(BF16), v_ref[pl.ds(start, bk), cols])
            later_ref[h, r0:, :] = later + jnp.sum(stay, axis=1, keepdims=True)

    n_diag = tq // bk
    for d in reversed(range(n_diag)):
        tile(i * n_diag + d, d * bk)

    def body(jj, c):
        tile(i * n_diag - 1 - jj, None)
        return c

    lax.fori_loop(0, i * n_diag, body, 0)
    for h in range(heads):
        o_ref[:, h * HEAD_DIM:(h + 1) * HEAD_DIM] = acc_ref[h].astype(o_ref.dtype)


def _stickbreak(proj, seq, batch, gw, col0):
    t = proj.shape[0]
    tq, bk = 512, 256
    nq = seq // tq
    hb = 2
    wb = hb * HEAD_DIM
    c0 = col0 * HEAD_DIM // wb
    per = gw // wb
    ii = np.arange(bk)
    tri = jnp.asarray(ii[:, None] >= ii[None, :], BF16)
    kv_spec = lambda n: pl.BlockSpec((seq, wb), lambda b, h, i: (b, c0 + n * per + h))
    return pl.pallas_call(
        functools.partial(_stickbreak_body, bk=bk, heads=hb),
        grid=(batch, per, nq),
        in_specs=[pl.BlockSpec((tq, wb), lambda b, h, i: (b * nq + i, c0 + h)),
                  kv_spec(1), kv_spec(2),
                  pl.BlockSpec((bk, bk), lambda b, h, i: (0, 0))],
        out_specs=pl.BlockSpec((tq, wb), lambda b, h, i: (b * nq + i, h)),
        out_shape=jax.ShapeDtypeStruct((t, gw), BF16),
        scratch_shapes=[pltpu.VMEM((hb, tq, HEAD_DIM), F32), pltpu.VMEM((hb, tq, 1), F32)],
        compiler_params=_params("parallel", "parallel", "arbitrary"),
        name="stickbreak",
    )(proj, proj, proj, tri)


def _rglru_body(gate_ref, rec_ref, cw_ref, cb_ref, wa_ref, ba_ref, wx_ref, bx_ref, lam_ref, o_ref,
                xbuf, abuf, ubuf, hprev, *, blocks):
    i = pl.program_id(1)
    ts = rec_ref.shape[0]
    gw = rec_ref.shape[1]

    @pl.when(i == 0)
    def _():
        xbuf[0:SUBLANES, :] = jnp.zeros((SUBLANES, gw), F32)
        hprev[...] = jnp.zeros(hprev.shape, F32)

    x = rec_ref[...].astype(F32)
    xbuf[SUBLANES:SUBLANES + ts, :] = x
    cw = cw_ref[0]
    xr = cb_ref[0] + cw[3:4, :] * x
    for k in range(LRU_CONV - 1):
        off = SUBLANES - (LRU_CONV - 1) + k
        xr = xr + cw[k:k + 1, :] * xbuf[off:off + ts, :]
    xbuf[0:SUBLANES, :] = x[ts - SUBLANES:ts, :]

    log_sig_lam = -_softplus(-lam_ref[0])
    xrb = xr.astype(BF16)
    bw = gw // blocks
    first = (lax.broadcasted_iota(jnp.int32, (ts, bw), 0) == 0) & (i == 0)
    for n in range(blocks):
        cols = slice(n * bw, (n + 1) * bw)
        r = _sigmoid(_dot(xrb[:, cols], wa_ref[0, n]) + ba_ref[0, :, cols])
        gi = _sigmoid(_dot(xrb[:, cols], wx_ref[0, n]) + bx_ref[0, :, cols])
        log_a = LRU_C * r * log_sig_lam[:, cols]
        mult = jnp.where(first, 1.0, jnp.sqrt(jnp.maximum(1.0 - jnp.exp(2.0 * log_a), 0.0)))
        abuf[:, cols] = jnp.exp(log_a)
        ubuf[:, cols] = mult * (gi * xr[:, cols])

    rows = lax.broadcasted_iota(jnp.int32, (SUBLANES, gw), 0)

    def group(gidx, h):
        start = pl.multiple_of(gidx * SUBLANES, SUBLANES)
        a = abuf[pl.ds(start, SUBLANES), :]
        u = ubuf[pl.ds(start, SUBLANES), :]
        for s in (1, 2, 4):
            keep = rows >= s
            a_sh = jnp.where(keep, pltpu.roll(a, s, axis=0), 1.0)
            u_sh = jnp.where(keep, pltpu.roll(u, s, axis=0), 0.0)
            u = u + a * u_sh
            a = a * a_sh
        hs = u + a * h
        ubuf[pl.ds(start, SUBLANES), :] = hs
        return jnp.broadcast_to(hs[SUBLANES - 1:SUBLANES, :], (SUBLANES, gw))

    hprev[...] = lax.fori_loop(0, ts // SUBLANES, group, hprev[...])
    o_ref[...] = (ubuf[...] * _gelu_tanh(gate_ref[...].astype(F32))).astype(o_ref.dtype)


def _rglru(proj, cw, cb3, wa, ba3, wx, bx3, lam3, layer, seq, batch, gw, col0):
    t = proj.shape[0]
    blocks = gw // HEAD_DIM
    ts = 256
    ns = seq // ts
    cpb = col0 // blocks
    vec_spec = pl.BlockSpec((1, 1, gw), lambda b, i: (layer, 0, 0))
    w_spec = pl.BlockSpec((1, blocks, HEAD_DIM, HEAD_DIM), lambda b, i: (layer, 0, 0, 0))
    return pl.pallas_call(
        functools.partial(_rglru_body, blocks=blocks),
        grid=(batch, ns),
        in_specs=[pl.BlockSpec((ts, gw), lambda b, i: (b * ns + i, cpb)),
                  pl.BlockSpec((ts, gw), lambda b, i: (b * ns + i, cpb + 1)),
                  pl.BlockSpec((1, LRU_CONV, gw), lambda b, i: (layer, 0, 0)),
                  vec_spec, w_spec, vec_spec, w_spec, vec_spec, vec_spec],
        out_specs=pl.BlockSpec((ts, gw), lambda b, i: (b * ns + i, 0)),
        out_shape=jax.ShapeDtypeStruct((t, gw), BF16),
        scratch_shapes=[pltpu.VMEM((SUBLANES + ts, gw), F32), pltpu.VMEM((ts, gw), F32),
                        pltpu.VMEM((ts, gw), F32), pltpu.VMEM((SUBLANES, gw), F32)],
        compiler_params=_params("parallel", "arbitrary"),
        name="rglru",
    )(proj, proj, cw, cb3, wa, ba3, wx, bx3, lam3)


def _rope_tables(pos):
    half = ROPE_DIMS // 2
    inv = ROPE_THETA ** (-jnp.arange(half, dtype=F32) / half)
    ang = pos.astype(F32)[:, None] * inv[None, :]
    cos, sin = jnp.cos(ang), jnp.sin(ang)
    n = pos.shape[0]
    zeros = jnp.zeros((n, half), F32)
    rest = HEAD_DIM - 2 * half
    c = jnp.concatenate([cos, cos, jnp.ones((n, rest), F32)], axis=1)
    s_lo = jnp.concatenate([-sin, zeros, jnp.zeros((n, rest), F32)], axis=1)
    s_hi = jnp.concatenate([zeros, sin, jnp.zeros((n, rest), F32)], axis=1)
    return c, s_lo, s_hi


def _partial_rope(x, c, s_lo, s_hi):
    half = ROPE_DIMS // 2
    return x * c + pltpu.roll(x, HEAD_DIM - half, axis=1) * s_lo + pltpu.roll(x, half, axis=1) * s_hi


def _nsa_prep_body(q_ref, ks_ref, kw_ref, qw_ref, kwt_ref, c_ref, lo_ref, hi_ref, qo_ref, ko_ref,
                   *, heads, kv_heads):
    c, s_lo, s_hi = c_ref[...], lo_ref[...], hi_ref[...]

    def prep(x, w):
        return _partial_rope(_rms_rows(x) * w, c, s_lo, s_hi)

    qw = qw_ref[0]
    for h in range(heads):
        cols = slice(h * HEAD_DIM, (h + 1) * HEAD_DIM)
        qo_ref[:, cols] = (prep(q_ref[:, cols].astype(F32), qw) * (HEAD_DIM ** -0.5)).astype(qo_ref.dtype)
    for n, src in enumerate((ks_ref, kw_ref)):
        w = kwt_ref[0, n + 1:n + 2, :]
        for g in range(kv_heads):
            cols = slice(g * HEAD_DIM, (g + 1) * HEAD_DIM)
            dst = slice((n * kv_heads + g) * HEAD_DIM, (n * kv_heads + g + 1) * HEAD_DIM)
            ko_ref[:, dst] = prep(src[:, cols].astype(F32), w).astype(ko_ref.dtype)


def _nsa_prep(proj, qw3, kw3, tables, layer, seq, batch, gw, colq, colkv):
    t = proj.shape[0]
    heads = gw // HEAD_DIM
    kvw = NSA_KV_HEADS * HEAD_DIM
    ts = 256
    ns = seq // ts
    rope_spec = pl.BlockSpec((ts, HEAD_DIM), lambda b, i: (i, 0))
    cq = colq * HEAD_DIM // gw
    ck = colkv * HEAD_DIM // kvw
    return pl.pallas_call(
        functools.partial(_nsa_prep_body, heads=heads, kv_heads=NSA_KV_HEADS),
        grid=(batch, ns),
        in_specs=[pl.BlockSpec((ts, gw), lambda b, i: (b * ns + i, cq)),
                  pl.BlockSpec((ts, kvw), lambda b, i: (b * ns + i, ck + 2)),
                  pl.BlockSpec((ts, kvw), lambda b, i: (b * ns + i, ck + 4)),
                  pl.BlockSpec((1, 1, HEAD_DIM), lambda b, i: (layer, 0, 0)),
                  pl.BlockSpec((1, 3, HEAD_DIM), lambda b, i: (layer, 0, 0)),
                  rope_spec, rope_spec, rope_spec],
        out_specs=[pl.BlockSpec((ts, gw), lambda b, i: (b * ns + i, 0)),
                   pl.BlockSpec((ts, 2 * kvw), lambda b, i: (b * ns + i, 0))],
        out_shape=[jax.ShapeDtypeStruct((t, gw), BF16), jax.ShapeDtypeStruct((t, 2 * kvw), BF16)],
        compiler_params=_params("parallel", "parallel"),
        name="nsa_prep",
    )(proj, proj, proj, qw3, kw3, *tables)


def _compress_body(seg_ref, pos_ref, w1_ref, w2_ref, kw_ref, c_ref, lo_ref, hi_ref, o_ref):
    seg = seg_ref[0, 0]
    ns = seg.shape[0]
    first = _dot((seg + pos_ref[0, 0, 0:1, :]).astype(BF16), w1_ref[0, 0, 0])
    second = _dot((seg + pos_ref[0, 0, 1:2, :]).astype(BF16), w1_ref[0, 0, 1])
    hid = _gelu_tanh(first + pltpu.roll(second, ns - 1, axis=0))
    out = _dot(hid.astype(BF16), w2_ref[0, 0])

    @pl.when(pl.program_id(1) < NSA_KV_HEADS)
    def _():
        o_ref[0, 0] = _partial_rope(_rms_rows(out) * kw_ref[0, 0:1, :], c_ref[...], lo_ref[...], hi_ref[...])

    @pl.when(pl.program_id(1) >= NSA_KV_HEADS)
    def _():
        o_ref[0, 0] = out


def _compress(segs, pos4, w1, w2, kw3, tables, layer):
    batch, n4, ns, width = segs.shape
    hidden = w1.shape[-1]
    g = NSA_KV_HEADS
    full = pl.BlockSpec((ns, HEAD_DIM), lambda b, c: (0, 0))
    return pl.pallas_call(
        _compress_body,
        grid=(batch, n4),
        in_specs=[pl.BlockSpec((1, 1, ns, width), lambda b, c: (b, c, 0, 0)),
                  pl.BlockSpec((1, 1, 2, width), lambda b, c: (layer, c // g, 0, 0)),
                  pl.BlockSpec((1, 1, 2, width, hidden), lambda b, c: (layer, c // g, 0, 0, 0)),
                  pl.BlockSpec((1, 1, hidden, HEAD_DIM), lambda b, c: (layer, c // g, 0, 0)),
                  pl.BlockSpec((1, 3, HEAD_DIM), lambda b, c: (layer, 0, 0)),
                  full, full, full],
        out_specs=pl.BlockSpec((1, 1, ns, HEAD_DIM), lambda b, c: (b, c, 0, 0)),
        out_shape=jax.ShapeDtypeStruct((batch, n4, ns, HEAD_DIM), F32),
        compiler_params=_params("parallel", "parallel"),
        name="nsa_compress",
    )(segs, pos4, w1, w2, kw3, *tables)


def _cmp_select_body(q_ref, kc_ref, vc_ref, ov_ref, o_ref, sel_ref, score_t, *, rep, n_cmp, n_slc, top_k):
    i = pl.program_id(2)
    tq = q_ref.shape[0]
    ncp = kc_ref.shape[2]
    nsp = ov_ref.shape[1]
    kc = kc_ref[0, 0].astype(BF16)
    vc = vc_ref[0, 0].astype(BF16)
    tpos = i * tq + lax.broadcasted_iota(jnp.int32, (tq, ncp), 0)
    blk_n = lax.broadcasted_iota(jnp.int32, (tq, ncp), 1)
    visible = (blk_n * CMP_STRIDE + (CMP_LEN - 1) <= tpos) & (blk_n < n_cmp)
    p_sum = jnp.zeros((tq, ncp), F32)
    for r in range(rep):
        cols = slice(r * HEAD_DIM, (r + 1) * HEAD_DIM)
        s = jnp.where(visible, _dot_nt(q_ref[:, cols], kc), -jnp.inf)
        m = jnp.max(s, axis=-1, keepdims=True)
        m = jnp.where(m > -jnp.inf, m, 0.0)
        e = jnp.where(visible, jnp.exp(s - m), 0.0)
        p = e / jnp.maximum(jnp.sum(e, axis=-1, keepdims=True), 1e-30)
        o_ref[:, cols] = _dot(p.astype(BF16), vc)
        p_sum = p_sum + p
    p_slc = _dot_split(p_sum, ov_ref[...])

    tpos_s = i * tq + lax.broadcasted_iota(jnp.int32, (tq, nsp), 0)
    blk_s = lax.broadcasted_iota(jnp.int32, (tq, nsp), 1)
    cur = tpos_s // SLC_LEN
    forced = (blk_s == 0) | (blk_s == cur) | (blk_s == cur - 1)
    valid = blk_s <= cur
    score = jnp.where(forced, FORCED_SCORE, jnp.where(valid, p_slc, NEG_BIG))
    score_t[...] = score.T
    n_rows = _round_up(n_slc, SUBLANES)
    mine = score_t[0:n_rows, :]
    my_blk = lax.broadcasted_iota(jnp.int32, (n_rows, tq), 0)

    def count(other, rank):
        row = score_t[pl.ds(other, 1), :]
        strictly = jnp.where(row > mine, 1.0, 0.0)
        or_equal = jnp.where(row >= mine, 1.0, 0.0)
        return rank + jnp.where(my_blk > other, or_equal, strictly)

    rank = lax.fori_loop(0, n_slc, count, jnp.zeros((n_rows, tq), F32), unroll=4)
    score_t[0:n_rows, :] = jnp.where(rank < top_k, 1.0, 0.0)
    chosen = score_t[...].T
    sel_ref[0] = jnp.where(valid, chosen, 0.0).astype(sel_ref.dtype)


def _cmp_select(qn, cmp_kv, overlap, seq, batch, gw):
    t = qn.shape[0]
    g = NSA_KV_HEADS
    rep = gw // HEAD_DIM // g
    ncp = cmp_kv.shape[2]
    nsp = overlap.shape[1]
    n_slc = seq // SLC_LEN
    tq = 256
    nq = seq // tq
    body = functools.partial(_cmp_select_body, rep=rep, n_cmp=(seq - CMP_LEN) // CMP_STRIDE + 1,
                             n_slc=n_slc, top_k=min(SLC_TOPK, n_slc))
    return pl.pallas_call(
        body,
        grid=(batch, g, nq),
        in_specs=[pl.BlockSpec((tq, rep * HEAD_DIM), lambda b, gi, i: (b * nq + i, gi)),
                  pl.BlockSpec((1, 1, ncp, HEAD_DIM), lambda b, gi, i: (b, gi, 0, 0)),
                  pl.BlockSpec((1, 1, ncp, HEAD_DIM), lambda b, gi, i: (b, g + gi, 0, 0)),
                  pl.BlockSpec((ncp, nsp), lambda b, gi, i: (0, 0))],
        out_specs=[pl.BlockSpec((tq, rep * HEAD_DIM), lambda b, gi, i: (b * nq + i, gi)),
                   pl.BlockSpec((1, tq, nsp), lambda b, gi, i: (gi, b * nq + i, 0))],
        out_shape=[jax.ShapeDtypeStruct((t, gw), F32), jax.ShapeDtypeStruct((g, t, nsp), BF16)],
        scratch_shapes=[pltpu.VMEM((nsp, tq), F32)],
        compiler_params=_params("parallel", "parallel", "parallel"),
        name="nsa_cmp_select",
    )(qn, cmp_kv, cmp_kv, overlap)


def _attend_body(q_ref, ks_ref, vs_ref, kw_ref, vw_ref, sel_ref, ex_ref, oc_ref, gl_ref, o_ref,
                 s_ref, m_ref, l_ref, acc_ref, *, rep, groups, blk, span):
    i = pl.program_id(1)
    tq = q_ref.shape[0]
    rows = rep * tq
    folds = blk // LANES
    n_blocks = ((i + 1) * tq + blk - 1) // blk

    def stacked_q(g):
        return jnp.concatenate(
            [q_ref[:, (g * rep + r) * HEAD_DIM:(g * rep + r + 1) * HEAD_DIM] for r in range(rep)], axis=0)

    m_ref[...] = jnp.full(m_ref.shape, NEG_BIG, F32)
    l_ref[...] = jnp.zeros(l_ref.shape, F32)
    acc_ref[...] = jnp.zeros(acc_ref.shape, F32)
    tpos = i * tq + lax.broadcasted_iota(jnp.int32, (tq, blk), 0)
    kcol = lax.broadcasted_iota(jnp.int32, (tq, blk), 1)

    def scores(j, carry):
        start = pl.multiple_of(j * blk, blk)
        causal = start + kcol <= tpos
        for g in range(groups):
            cols = slice(g * HEAD_DIM, (g + 1) * HEAD_DIM)
            picked = _dot(sel_ref[g], ex_ref[j])
            bias = jnp.where((picked > 0.5) & causal, 0.0, NEG_BIG)
            s = _dot_nt(stacked_q(g), ks_ref[pl.ds(start, blk), cols]) + jnp.concatenate([bias] * rep, axis=0)
            s_ref[g, j] = s
            m = m_ref[g]
            for c in range(folds):
                m = jnp.maximum(m, s[:, c * LANES:(c + 1) * LANES])
            m_ref[g] = m
        return carry

    lax.fori_loop(0, n_blocks, scores, 0)

    for g in range(groups):
        m_ref[g] = jnp.broadcast_to(jnp.max(m_ref[g], axis=-1, keepdims=True), (rows, LANES))

    def weights(j, carry):
        start = pl.multiple_of(j * blk, blk)
        for g in range(groups):
            cols = slice(g * HEAD_DIM, (g + 1) * HEAD_DIM)
            p = jnp.exp(s_ref[g, j] - jnp.concatenate([m_ref[g]] * folds, axis=1))
            l = l_ref[g]
            for c in range(folds):
                l = l + p[:, c * LANES:(c + 1) * LANES]
            l_ref[g] = l
            acc_ref[g] += _dot(p.astype(BF16), vs_ref[pl.ds(start, blk), cols])
        return carry

    lax.fori_loop(0, n_blocks, weights, 0)

    wstart = pl.multiple_of(jnp.maximum(i * tq + tq - span, 0), tq)
    wpos = i * tq + (lax.broadcasted_iota(jnp.int32, (rep * tq, span), 0) & (tq - 1))
    dist = wpos - (wstart + lax.broadcasted_iota(jnp.int32, (rep * tq, span), 1))
    wbias = jnp.where((dist >= 0) & (dist < WINDOW), 0.0, NEG_BIG)
    gates = _sigmoid(gl_ref[...])
    lane = lax.broadcasted_iota(jnp.int32, gates.shape, 1)

    def gate(head, branch):
        return jnp.sum(jnp.where(lane == 3 * head + branch, gates, 0.0), axis=-1, keepdims=True)

    for g in range(groups):
        cols = slice(g * HEAD_DIM, (g + 1) * HEAD_DIM)
        s = _dot_nt(stacked_q(g), kw_ref[pl.ds(wstart, span), cols]) + wbias
        e = jnp.exp(s - jnp.max(s, axis=-1, keepdims=True))
        o_win = _dot(e.astype(BF16), vw_ref[pl.ds(wstart, span), cols]) / jnp.sum(e, axis=-1, keepdims=True)
        o_slc = acc_ref[g] / jnp.sum(l_ref[g], axis=-1, keepdims=True)
        for r in range(rep):
            head = g * rep + r
            hc = slice(head * HEAD_DIM, (head + 1) * HEAD_DIM)
            rows = slice(r * tq, (r + 1) * tq)
            mix = gate(head, 0) * oc_ref[:, hc] + gate(head, 1) * o_slc[rows, :] + gate(head, 2) * o_win[rows, :]
            o_ref[:, hc] = mix.astype(o_ref.dtype)


def _attend(qn, kk, proj, sel, expand, o_cmp, gl, seq, batch, gw, colkv):
    t = qn.shape[0]
    g = NSA_KV_HEADS
    kvw = g * HEAD_DIM
    rep = gw // HEAD_DIM // g
    nsp = sel.shape[2]
    tq = 128
    blk = expand.shape[2]
    nq = seq // tq
    span = WINDOW + tq
    ck = colkv * HEAD_DIM // kvw
    row_spec = pl.BlockSpec((tq, gw), lambda b, i: (b * nq + i, 0))
    return pl.pallas_call(
        functools.partial(_attend_body, rep=rep, groups=g, blk=blk, span=span),
        grid=(batch, nq),
        in_specs=[row_spec,
                  pl.BlockSpec((seq, kvw), lambda b, i: (b, 0)),
                  pl.BlockSpec((seq, kvw), lambda b, i: (b, ck + 3)),
                  pl.BlockSpec((seq, kvw), lambda b, i: (b, 1)),
                  pl.BlockSpec((seq, kvw), lambda b, i: (b, ck + 5)),
                  pl.BlockSpec((g, tq, nsp), lambda b, i: (0, b * nq + i, 0)),
                  pl.BlockSpec((seq // blk, nsp, blk), lambda b, i: (0, 0, 0)),
                  row_spec,
                  pl.BlockSpec((tq, LANES), lambda b, i: (b * nq + i, 0))],
        out_specs=row_spec,
        out_shape=jax.ShapeDtypeStruct((t, gw), BF16),
        scratch_shapes=[pltpu.VMEM((g, seq // blk, rep * tq, blk), F32),
                        pltpu.VMEM((g, rep * tq, LANES), F32), pltpu.VMEM((g, rep * tq, LANES), F32),
                        pltpu.VMEM((g, rep * tq, HEAD_DIM), F32)],
        compiler_params=_params("parallel", "parallel"),
        name="nsa_attend",
    )(qn, kk, proj, kk, proj, sel, expand, o_cmp, gl)


def _nsa(proj, gl, qw3, kw3, pos4, w1, w2, layer, seq, batch, gw, colq, colkv):
    g = NSA_KV_HEADS
    kvw = g * HEAD_DIM
    tables = _rope_tables(jnp.arange(seq))
    qn, kk = _nsa_prep(proj, qw3, kw3, tables, layer, seq, batch, gw, colq, colkv)

    ns = seq // CMP_STRIDE
    n_cmp = (seq - CMP_LEN) // CMP_STRIDE + 1
    c0 = colkv * HEAD_DIM
    segs = proj[:, c0:c0 + 2 * kvw].reshape(batch, ns, CMP_STRIDE, 2 * g, HEAD_DIM)
    segs = segs.transpose(0, 3, 1, 2, 4).reshape(batch, 2 * g, ns, CMP_STRIDE * HEAD_DIM)
    cmp_end = jnp.arange(ns) * CMP_STRIDE + CMP_LEN - 1
    cmp_kv = _compress(segs, pos4, w1, w2, kw3, _rope_tables(cmp_end), layer)

    n_slc = seq // SLC_LEN
    nsp = _round_up(n_slc, LANES)
    ci = np.arange(ns)[:, None] * CMP_STRIDE
    sj = np.arange(nsp)[None, :] * SLC_LEN
    ov = np.maximum(0, np.minimum(ci + CMP_LEN, sj + SLC_LEN) - np.maximum(ci, sj)) / CMP_STRIDE
    ov = np.where((np.arange(ns)[:, None] < n_cmp) & (np.arange(nsp)[None, :] < n_slc), ov, 0.0)
    o_cmp, sel = _cmp_select(qn, cmp_kv, jnp.asarray(ov, BF16), seq, batch, gw)

    blk = 512
    key_blk = (np.arange(seq) // SLC_LEN).reshape(seq // blk, 1, blk)
    expand = jnp.asarray(np.arange(nsp)[None, :, None] == key_blk, BF16)
    return _attend(qn, kk, proj, sel, expand, o_cmp, gl, seq, batch, gw, colkv)


def kernel(x, attn_norm_w, w_in, lru_conv_w, lru_conv_b, lru_w_a, lru_b_a, lru_w_x, lru_b_x, lru_lambda,
           nsa_q_norm_w, nsa_k_norm_w, nsa_cmp_pos, nsa_cmp_w1, nsa_cmp_w2, w_out, mlp_norm_w,
           w_gate, w_up, mlp_conv_w, mlp_conv_b, w_down):
    batch, seq, d = x.shape
    depth = w_in.shape[0]
    gw = d // N_MIXERS
    gh = gw // HEAD_DIM
    kvw = NSA_KV_HEADS * HEAD_DIM
    n_main = 10 * gw + 6 * kvw
    n_gate = w_in.shape[2] - n_main
    t = batch * seq
    assert seq % 512 == 0 and seq >= WINDOW + 128 and gw % (NSA_KV_HEADS * HEAD_DIM) == 0
    assert n_main % 512 == 0 and n_gate <= LANES and d % 1024 == 0 and w_gate.shape[2] % LANES == 0

    w_in_b = w_in.astype(BF16)
    w_out_b = w_out.astype(BF16)
    w_gate_b = w_gate.astype(BF16)
    w_up_b = w_up.astype(BF16)
    w_down_b = w_down.astype(BF16)
    mlp_cb3 = mlp_conv_b[:, None, :]
    attn_w3 = attn_norm_w[:, None, :]
    mlp_w3 = mlp_norm_w[:, None, :]
    lru_cb3, lru_ba3, lru_bx3, lru_lam3 = (v[:, None, :] for v in (lru_conv_b, lru_b_a, lru_b_x, lru_lambda))
    lru_wa_b = lru_w_a.astype(BF16)
    lru_wx_b = lru_w_x.astype(BF16)
    qw3 = nsa_q_norm_w[:, None, :]
    seg_w = CMP_STRIDE * HEAD_DIM
    pos4 = nsa_cmp_pos.reshape(depth, 2, CMP_LEN // CMP_STRIDE, seg_w)
    cmp_w1 = nsa_cmp_w1.reshape(depth, 2, CMP_LEN // CMP_STRIDE, seg_w, -1).astype(BF16)
    cmp_w2 = nsa_cmp_w2.astype(BF16)

    xf = x.reshape(t, d)
    for layer in range(depth):
        hn = _rmsnorm(xf, attn_w3, layer)
        proj = _matmul(hn, w_in_b, layer, 0, n_main, 1024, 512, BF16, "inproj")
        gl = _matmul(hn, w_in_b, layer, n_main // LANES, LANES, 1024, LANES, F32, "inproj_gates")
        y_ret = _retention(proj, seq, batch, gw)
        y_sb = _stickbreak(proj, seq, batch, gw, 4 * gh)
        y_lru = _rglru(proj, lru_conv_w, lru_cb3, lru_wa_b, lru_ba3, lru_wx_b, lru_bx3, lru_lam3,
                       layer, seq, batch, gw, 7 * gh)
        y_nsa = _nsa(proj, gl, qw3, nsa_k_norm_w, pos4, cmp_w1, cmp_w2, layer, seq, batch, gw,
                     9 * gh, 10 * gh)
        xf = _outproj((y_ret, y_sb, y_lru, y_nsa), w_out_b, xf, layer, 1024, 512)
        hn = _rmsnorm(xf, mlp_w3, layer)
        hid = _gateup(hn, w_gate_b, w_up_b, mlp_conv_w, mlp_cb3, layer, seq, 512, 1024)
        xf = _down(hid, w_down_b, xf, layer, 256, 1024)
    return xf.reshape(batch, seq, d)
```

```python
import functools
import math

import numpy as np
import jax
import jax.numpy as jnp
from jax import lax
from jax.experimental import pallas as pl
from jax.experimental.pallas import tpu as pltpu

F32 = jnp.float32
BF16 = jnp.bfloat16

HEAD_DIM = 128
N_MIXERS = 4
RET_CHUNK = 128
RET_ROT_BASE = 10000.0
LRU_CONV = 4
LRU_C = 8.0
NSA_KV_HEADS = 2
CMP_LEN = 32
CMP_STRIDE = 16
SLC_LEN = 64
SLC_TOPK = 16
WINDOW = 512
ROPE_THETA = 500000.0
ROPE_DIMS = HEAD_DIM // 4
MLP_CONV = 3
NORM_EPS = 1e-6

LANES = 128
SUBLANES = 8
VMEM_LIMIT_BYTES = 56 * 1024 * 1024
NEG_BIG = -1e30
FORCED_SCORE = 1e30


def _params(*sem, flags=None):
    return pltpu.CompilerParams(dimension_semantics=sem, vmem_limit_bytes=VMEM_LIMIT_BYTES, flags=flags)


def _round_up(x, m):
    return (x + m - 1) // m * m


def _dot(a, b):
    return jnp.dot(a, b, preferred_element_type=F32)


def _dot_nt(a, b):
    return lax.dot_general(a, b, (((1,), (1,)), ((), ())), preferred_element_type=F32)


def _dot_split(x, m):
    hi = x.astype(BF16)
    lo = (x - hi.astype(F32)).astype(BF16)
    return _dot(hi, m) + _dot(lo, m)


def _sigmoid(x):
    return 1.0 / (1.0 + jnp.exp(-x))


def _softplus(x):
    return jnp.maximum(x, 0.0) + jnp.log1p(jnp.exp(-jnp.abs(x)))


def _gelu_tanh(x):
    return x * (0.5 * (1.0 + jnp.tanh(math.sqrt(2.0 / math.pi) * (x + 0.044715 * (x * x * x)))))


def _rms_rows(x):
    return x * lax.rsqrt(jnp.mean(x * x, axis=-1, keepdims=True) + NORM_EPS)


def _rmsnorm_body(x_ref, w_ref, o_ref):
    o_ref[...] = (_rms_rows(x_ref[...]) * w_ref[0]).astype(o_ref.dtype)


def _rmsnorm(x, w3, layer):
    t, d = x.shape
    tr = 256
    return pl.pallas_call(
        _rmsnorm_body,
        grid=(t // tr,),
        in_specs=[pl.BlockSpec((tr, d), lambda i: (i, 0)),
                  pl.BlockSpec((1, 1, d), lambda i: (layer, 0, 0))],
        out_specs=pl.BlockSpec((tr, d), lambda i: (i, 0)),
        out_shape=jax.ShapeDtypeStruct((t, d), BF16),
        compiler_params=_params("parallel"),
        name="rmsnorm",
    )(x, w3)


def _matmul_body(a_ref, w_ref, o_ref):
    o_ref[...] = _dot(a_ref[...], w_ref[0]).astype(o_ref.dtype)


def _matmul(a, w, layer, col0, n, tm, tn, out_dtype, name):
    t, k = a.shape
    return pl.pallas_call(
        _matmul_body,
        grid=(t // tm, n // tn),
        in_specs=[pl.BlockSpec((tm, k), lambda i, j: (i, 0)),
                  pl.BlockSpec((1, k, tn), lambda i, j: (layer, 0, col0 + j))],
        out_specs=pl.BlockSpec((tm, tn), lambda i, j: (i, j)),
        out_shape=jax.ShapeDtypeStruct((t, n), out_dtype),
        compiler_params=_params("parallel", "parallel"),
        name=name,
    )(a, w)


def _outproj_body(y0_ref, y1_ref, y2_ref, y3_ref, w_ref, x_ref, o_ref):
    gw = y0_ref.shape[1]
    acc = x_ref[...]
    for n, y_ref in enumerate((y0_ref, y1_ref, y2_ref, y3_ref)):
        acc = acc + _dot(y_ref[...], w_ref[0, n * gw:(n + 1) * gw, :])
    o_ref[...] = acc


def _outproj(ys, w, x, layer, tm, tn):
    t, d = x.shape
    gw = ys[0].shape[1]
    y_spec = pl.BlockSpec((tm, gw), lambda i, j: (i, 0))
    return pl.pallas_call(
        _outproj_body,
        grid=(t // tm, d // tn),
        in_specs=[y_spec, y_spec, y_spec, y_spec,
                  pl.BlockSpec((1, N_MIXERS * gw, tn), lambda i, j: (layer, 0, j)),
                  pl.BlockSpec((tm, tn), lambda i, j: (i, j))],
        out_specs=pl.BlockSpec((tm, tn), lambda i, j: (i, j)),
        out_shape=jax.ShapeDtypeStruct((t, d), F32),
        compiler_params=_params("parallel", "parallel"),
        name="outproj",
    )(*ys, w, x)


def _gateup_body(a_ref, wg_ref, wu_ref, cw_ref, cb_ref, o_ref, gbuf, *, tiles_per_seq, sub):
    i = pl.program_id(1)
    tm = a_ref.shape[0]
    tn = o_ref.shape[1]

    @pl.when(i % tiles_per_seq == 0)
    def _():
        gbuf[0:SUBLANES, :] = jnp.zeros((SUBLANES, tn), F32)

    for c in range(tn // sub):
        cols = slice(c * sub, (c + 1) * sub)
        g = _dot(a_ref[...], wg_ref[0, :, cols])
        u = _dot(a_ref[...], wu_ref[0, :, cols])
        gbuf[SUBLANES:SUBLANES + tm, cols] = g
        cw = cw_ref[0, :, cols]
        gt = (cb_ref[0, :, cols] + cw[2:3, :] * g
              + cw[1:2, :] * gbuf[SUBLANES - 1:SUBLANES - 1 + tm, cols]
              + cw[0:1, :] * gbuf[SUBLANES - 2:SUBLANES - 2 + tm, cols])
        o_ref[:, cols] = (gt * _sigmoid(gt) * u).astype(o_ref.dtype)
        gbuf[0:SUBLANES, cols] = g[tm - SUBLANES:tm, :]


def _gateup(a, wg, wu, cw, cb3, layer, seq, tm, tn):
    t, k = a.shape
    f = wg.shape[2]
    w_spec = pl.BlockSpec((1, k, tn), lambda j, i: (layer, 0, j))
    return pl.pallas_call(
        functools.partial(_gateup_body, tiles_per_seq=seq // tm, sub=2 * LANES),
        grid=(pl.cdiv(f, tn), t // tm),
        in_specs=[pl.BlockSpec((tm, k), lambda j, i: (i, 0)), w_spec, w_spec,
                  pl.BlockSpec((1, MLP_CONV, tn), lambda j, i: (layer, 0, j)),
                  pl.BlockSpec((1, 1, tn), lambda j, i: (layer, 0, j))],
        out_specs=pl.BlockSpec((tm, tn), lambda j, i: (i, j)),
        out_shape=jax.ShapeDtypeStruct((t, f), BF16),
        scratch_shapes=[pltpu.VMEM((SUBLANES + tm, tn), F32)],
        compiler_params=_params("parallel", "arbitrary"),
        name="mlp_gateup",
    )(a, wg, wu, cw, cb3)


def _down_body(h_ref, w_ref, x_ref, o_ref):
    o_ref[...] = x_ref[...] + _dot(h_ref[...], w_ref[0])


def _down(h, w, x, layer, tm, tn):
    t, f = h.shape
    d = x.shape[1]
    return pl.pallas_call(
        _down_body,
        grid=(d // tn, t // tm),
        in_specs=[pl.BlockSpec((tm, f), lambda j, i: (i, 0)),
                  pl.BlockSpec((1, f, tn), lambda j, i: (layer, 0, j), pipeline_mode=pl.Buffered(1)),
                  pl.BlockSpec((tm, tn), lambda j, i: (i, j))],
        out_specs=pl.BlockSpec((tm, tn), lambda j, i: (i, j)),
        out_shape=jax.ShapeDtypeStruct((t, d), F32),
        compiler_params=_params("parallel", "parallel"),
        name="mlp_down",
    )(h, w, x)


def _retention_body(q_ref, k_ref, v_ref, g_ref, cos_ref, sin_ref, inner_ref, qd_ref, kd_ref, cd_ref,
                    o_ref, state, *, heads):
    @pl.when(pl.program_id(1) == 0)
    def _():
        state[...] = jnp.zeros(state.shape, F32)

    cos = cos_ref[...]
    sin = sin_ref[...]

    def rot(x):
        return x * cos + pltpu.roll(x, HEAD_DIM // 2, axis=1) * sin

    for h in range(heads):
        cols = slice(h * HEAD_DIM, (h + 1) * HEAD_DIM)
        q = rot(q_ref[:, cols].astype(F32))
        k = rot(k_ref[:, cols].astype(F32)) * (HEAD_DIM ** -0.5)
        vb = v_ref[:, cols]
        qb = q.astype(BF16)
        inner = _dot_nt(qb, k.astype(BF16)) * inner_ref[h]
        st = state[h]
        o = _dot(inner.astype(BF16), vb) + _dot(qb, st.astype(BF16)) * qd_ref[h]
        kd = (k * kd_ref[h]).T.astype(BF16)
        state[h] = st * cd_ref[h] + _dot(kd, vb)
        g = g_ref[:, cols].astype(F32)
        o_ref[:, cols] = (_rms_rows(o) * (g * _sigmoid(g))).astype(o_ref.dtype)


def _retention(proj, seq, batch, gw):
    t = proj.shape[0]
    heads = gw // HEAD_DIM
    c = RET_CHUNK
    nc = seq // c
    pos = jnp.arange(seq)
    inv = 1.0 / (RET_ROT_BASE ** jnp.linspace(0.0, 1.0, HEAD_DIM // 2, dtype=F32))
    ang = pos.astype(F32)[:, None] * inv[None, :]
    cos, sin = jnp.cos(ang), jnp.sin(ang)
    cos_t = jnp.concatenate([cos, cos], axis=1)
    sin_t = jnp.concatenate([-sin, sin], axis=1)
    log_g = jnp.log(1.0 - 2.0 ** (-5.0 - jnp.arange(heads, dtype=F32)))
    idx = jnp.arange(c, dtype=F32)
    rel = idx[:, None] - idx[None, :]
    inner_decay = jnp.where(rel >= 0, jnp.exp(log_g[:, None, None] * jnp.maximum(rel, 0.0)), 0.0)
    q_decay = jnp.broadcast_to(jnp.exp(log_g[:, None] * (idx + 1.0))[..., None], (heads, c, HEAD_DIM))
    k_decay = jnp.broadcast_to(jnp.exp(log_g[:, None] * (c - 1.0 - idx))[..., None], (heads, c, HEAD_DIM))
    chunk_decay = jnp.broadcast_to(jnp.exp(log_g * c)[:, None, None], (heads, HEAD_DIM, HEAD_DIM))

    def col(n):
        return pl.BlockSpec((c, gw), lambda b, i: (b * nc + i, n))

    rope_spec = pl.BlockSpec((c, HEAD_DIM), lambda b, i: (i, 0))
    const_spec = pl.BlockSpec((heads, c, HEAD_DIM), lambda b, i: (0, 0, 0))
    return pl.pallas_call(
        functools.partial(_retention_body, heads=heads),
        grid=(batch, nc),
        in_specs=[col(0), col(1), col(2), col(3), rope_spec, rope_spec,
                  const_spec, const_spec, const_spec, const_spec],
        out_specs=pl.BlockSpec((c, gw), lambda b, i: (b * nc + i, 0)),
        out_shape=jax.ShapeDtypeStruct((t, gw), BF16),
        scratch_shapes=[pltpu.VMEM((heads, HEAD_DIM, HEAD_DIM), F32)],
        compiler_params=_params("parallel", "arbitrary"),
        name="retention",
    )(proj, proj, proj, proj, cos_t, sin_t, inner_decay, q_decay, k_decay, chunk_decay)


def _stickbreak_body(q_ref, k_ref, v_ref, tri_ref, o_ref, acc_ref, later_ref, *, bk, heads):
    i = pl.program_id(2)
    tq = q_ref.shape[0]
    tri = tri_ref[...]
    log2e = 1.0 / math.log(2.0)
    acc_ref[...] = jnp.zeros(acc_ref.shape, F32)
    later_ref[...] = jnp.zeros(later_ref.shape, F32)
    sign_bit = jnp.uint32(0x80000000)

    def tile(j, row0):
        masked = row0 is not None
        r0 = row0 if masked else 0
        nr = tq - r0
        start = pl.multiple_of(j * bk, bk)
        if masked:
            qpos = i * tq + r0 + lax.broadcasted_iota(jnp.int32, (nr, bk), 0)
            past = start + lax.broadcasted_iota(jnp.int32, (nr, bk), 1) < qpos
        for h in range(heads):
            cols = slice(h * HEAD_DIM, (h + 1) * HEAD_DIM)
            y = _dot_nt(q_ref[r0:, cols], k_ref[pl.ds(start, bk), cols]) * (HEAD_DIM ** -0.5 * log2e)
            minus_abs = pltpu.bitcast(pltpu.bitcast(y, jnp.uint32) | sign_bit, F32)
            stay = jnp.maximum(y, 0.0) + jnp.log(1.0 + jnp.exp2(minus_abs)) * log2e
            if masked:
                stay = jnp.where(past, stay, 0.0)
            hi = stay.astype(BF16)
            both = _dot(hi, tri)
            later = later_ref[h, r0:, :]
            w = jnp.exp2(y - (both + later))
            if masked:
                w = jnp.where(past, w, 0.0)
            acc_ref[h, r0:, :] += _dot(w.astype(BF16), v_ref[pl.ds(start, bk), cols])
            later_ref[h, r0:, :] = later + jnp.sum(stay, axis=1, keepdims=True)

    n_diag = tq // bk
    for d in reversed(range(n_diag)):
        tile(i * n_diag + d, d * bk)

    def body(jj, c):
        tile(i * n_diag - 1 - jj, None)
        return c

    lax.fori_loop(0, i * n_diag, body, 0)
    for h in range(heads):
        o_ref[:, h * HEAD_DIM:(h + 1) * HEAD_DIM] = acc_ref[h].astype(o_ref.dtype)


def _stickbreak(proj, seq, batch, gw, col0):
    t = proj.shape[0]
    tq, bk = 512, 256
    nq = seq // tq
    hb = gw // HEAD_DIM
    wb = hb * HEAD_DIM
    c0 = col0 * HEAD_DIM // wb
    per = gw // wb
    ii = np.arange(bk)
    tri = jnp.asarray(ii[:, None] >= ii[None, :], BF16)
    kv_spec = lambda n: pl.BlockSpec((seq, wb), lambda b, h, i: (b, c0 + n * per + h))
    return pl.pallas_call(
        functools.partial(_stickbreak_body, bk=bk, heads=hb),
        grid=(batch, per, nq),
        in_specs=[pl.BlockSpec((tq, wb), lambda b, h, i: (b * nq + i, c0 + h)),
                  kv_spec(1), kv_spec(2),
                  pl.BlockSpec((bk, bk), lambda b, h, i: (0, 0))],
        out_specs=pl.BlockSpec((tq, wb), lambda b, h, i: (b * nq + i, h)),
        out_shape=jax.ShapeDtypeStruct((t, gw), BF16),
        scratch_shapes=[pltpu.VMEM((hb, tq, HEAD_DIM), F32), pltpu.VMEM((hb, tq, 1), F32)],
        compiler_params=_params("parallel", "parallel", "arbitrary"),
        name="stickbreak",
    )(proj, proj, proj, tri)


def _rglru_body(gate_ref, rec_ref, cw_ref, cb_ref, wa_ref, ba_ref, wx_ref, bx_ref, lam_ref, o_ref,
                xbuf, abuf, ubuf, hprev, *, blocks):
    i = pl.program_id(1)
    ts = rec_ref.shape[0]
    gw = rec_ref.shape[1]

    @pl.when(i == 0)
    def _():
        xbuf[0:SUBLANES, :] = jnp.zeros((SUBLANES, gw), F32)
        hprev[...] = jnp.zeros(hprev.shape, F32)

    x = rec_ref[...].astype(F32)
    xbuf[SUBLANES:SUBLANES + ts, :] = x
    cw = cw_ref[0]
    xr = cb_ref[0] + cw[3:4, :] * x
    for k in range(LRU_CONV - 1):
        off = SUBLANES - (LRU_CONV - 1) + k
        xr = xr + cw[k:k + 1, :] * xbuf[off:off + ts, :]
    xbuf[0:SUBLANES, :] = x[ts - SUBLANES:ts, :]

    log_sig_lam = -_softplus(-lam_ref[0])
    xrb = xr.astype(BF16)
    bw = gw // blocks
    first = (lax.broadcasted_iota(jnp.int32, (ts, bw), 0) == 0) & (i == 0)
    for n in range(blocks):
        cols = slice(n * bw, (n + 1) * bw)
        r = _sigmoid(_dot(xrb[:, cols], wa_ref[0, n]) + ba_ref[0, :, cols])
        gi = _sigmoid(_dot(xrb[:, cols], wx_ref[0, n]) + bx_ref[0, :, cols])
        log_a = LRU_C * r * log_sig_lam[:, cols]
        mult = jnp.where(first, 1.0, jnp.sqrt(jnp.maximum(1.0 - jnp.exp(2.0 * log_a), 0.0)))
        abuf[:, cols] = jnp.exp(log_a)
        ubuf[:, cols] = mult * (gi * xr[:, cols])

    rows = lax.broadcasted_iota(jnp.int32, (SUBLANES, gw), 0)

    def group(gidx, h):
        start = pl.multiple_of(gidx * SUBLANES, SUBLANES)
        a = abuf[pl.ds(start, SUBLANES), :]
        u = ubuf[pl.ds(start, SUBLANES), :]
        for s in (1, 2, 4):
            keep = rows >= s
            a_sh = jnp.where(keep, pltpu.roll(a, s, axis=0), 1.0)
            u_sh = jnp.where(keep, pltpu.roll(u, s, axis=0), 0.0)
            u = u + a * u_sh
            a = a * a_sh
        hs = u + a * h
        ubuf[pl.ds(start, SUBLANES), :] = hs
        return jnp.broadcast_to(hs[SUBLANES - 1:SUBLANES, :], (SUBLANES, gw))

    hprev[...] = lax.fori_loop(0, ts // SUBLANES, group, hprev[...])
    o_ref[...] = (ubuf[...] * _gelu_tanh(gate_ref[...].astype(F32))).astype(o_ref.dtype)


def _rglru(proj, cw, cb3, wa, ba3, wx, bx3, lam3, layer, seq, batch, gw, col0):
    t = proj.shape[0]
    blocks = gw // HEAD_DIM
    ts = 256
    ns = seq // ts
    cpb = col0 // blocks
    vec_spec = pl.BlockSpec((1, 1, gw), lambda b, i: (layer, 0, 0))
    w_spec = pl.BlockSpec((1, blocks, HEAD_DIM, HEAD_DIM), lambda b, i: (layer, 0, 0, 0))
    return pl.pallas_call(
        functools.partial(_rglru_body, blocks=blocks),
        grid=(batch, ns),
        in_specs=[pl.BlockSpec((ts, gw), lambda b, i: (b * ns + i, cpb)),
                  pl.BlockSpec((ts, gw), lambda b, i: (b * ns + i, cpb + 1)),
                  pl.BlockSpec((1, LRU_CONV, gw), lambda b, i: (layer, 0, 0)),
                  vec_spec, w_spec, vec_spec, w_spec, vec_spec, vec_spec],
        out_specs=pl.BlockSpec((ts, gw), lambda b, i: (b * ns + i, 0)),
        out_shape=jax.ShapeDtypeStruct((t, gw), BF16),
        scratch_shapes=[pltpu.VMEM((SUBLANES + ts, gw), F32), pltpu.VMEM((ts, gw), F32),
                        pltpu.VMEM((ts, gw), F32), pltpu.VMEM((SUBLANES, gw), F32)],
        compiler_params=_params("parallel", "arbitrary"),
        name="rglru",
    )(proj, proj, cw, cb3, wa, ba3, wx, bx3, lam3)


def _rope_tables(pos):
    half = ROPE_DIMS // 2
    inv = ROPE_THETA ** (-jnp.arange(half, dtype=F32) / half)
    ang = pos.astype(F32)[:, None] * inv[None, :]
    cos, sin = jnp.cos(ang), jnp.sin(ang)
    n = pos.shape[0]
    zeros = jnp.zeros((n, half), F32)
    rest = HEAD_DIM - 2 * half
    c = jnp.concatenate([cos, cos, jnp.ones((n, rest), F32)], axis=1)
    s_lo = jnp.concatenate([-sin, zeros, jnp.zeros((n, rest), F32)], axis=1)
    s_hi = jnp.concatenate([zeros, sin, jnp.zeros((n, rest), F32)], axis=1)
    return c, s_lo, s_hi


def _partial_rope(x, c, s_lo, s_hi):
    half = ROPE_DIMS // 2
    return x * c + pltpu.roll(x, HEAD_DIM - half, axis=1) * s_lo + pltpu.roll(x, half, axis=1) * s_hi


def _nsa_prep_body(q_ref, ks_ref, kw_ref, qw_ref, kwt_ref, c_ref, lo_ref, hi_ref, qo_ref, ko_ref,
                   *, heads, kv_heads):
    c, s_lo, s_hi = c_ref[...], lo_ref[...], hi_ref[...]

    def prep(x, w):
        return _partial_rope(_rms_rows(x) * w, c, s_lo, s_hi)

    qw = qw_ref[0]
    for h in range(heads):
        cols = slice(h * HEAD_DIM, (h + 1) * HEAD_DIM)
        qo_ref[:, cols] = (prep(q_ref[:, cols].astype(F32), qw) * (HEAD_DIM ** -0.5)).astype(qo_ref.dtype)
    for n, src in enumerate((ks_ref, kw_ref)):
        w = kwt_ref[0, n + 1:n + 2, :]
        for g in range(kv_heads):
            cols = slice(g * HEAD_DIM, (g + 1) * HEAD_DIM)
            dst = slice((n * kv_heads + g) * HEAD_DIM, (n * kv_heads + g + 1) * HEAD_DIM)
            ko_ref[:, dst] = prep(src[:, cols].astype(F32), w).astype(ko_ref.dtype)


def _nsa_prep(proj, qw3, kw3, tables, layer, seq, batch, gw, colq, colkv):
    t = proj.shape[0]
    heads = gw // HEAD_DIM
    kvw = NSA_KV_HEADS * HEAD_DIM
    ts = 256
    ns = seq // ts
    rope_spec = pl.BlockSpec((ts, HEAD_DIM), lambda b, i: (i, 0))
    cq = colq * HEAD_DIM // gw
    ck = colkv * HEAD_DIM // kvw
    return pl.pallas_call(
        functools.partial(_nsa_prep_body, heads=heads, kv_heads=NSA_KV_HEADS),
        grid=(batch, ns),
        in_specs=[pl.BlockSpec((ts, gw), lambda b, i: (b * ns + i, cq)),
                  pl.BlockSpec((ts, kvw), lambda b, i: (b * ns + i, ck + 2)),
                  pl.BlockSpec((ts, kvw), lambda b, i: (b * ns + i, ck + 4)),
                  pl.BlockSpec((1, 1, HEAD_DIM), lambda b, i: (layer, 0, 0)),
                  pl.BlockSpec((1, 3, HEAD_DIM), lambda b, i: (layer, 0, 0)),
                  rope_spec, rope_spec, rope_spec],
        out_specs=[pl.BlockSpec((ts, gw), lambda b, i: (b * ns + i, 0)),
                   pl.BlockSpec((ts, 2 * kvw), lambda b, i: (b * ns + i, 0))],
        out_shape=[jax.ShapeDtypeStruct((t, gw), BF16), jax.ShapeDtypeStruct((t, 2 * kvw), BF16)],
        compiler_params=_params("parallel", "parallel"),
        name="nsa_prep",
    )(proj, proj, proj, qw3, kw3, *tables)


def _compress_body(x_ref, pos_ref, w1_ref, w2_ref, kw_ref, c_ref, lo_ref, hi_ref, o_ref, xs):
    xs[...] = x_ref[...].astype(F32)
    ns = o_ref.shape[2]
    hidden = w2_ref.shape[2]
    first = jnp.zeros((ns, hidden), F32)
    second = jnp.zeros((ns, hidden), F32)
    for l in range(CMP_STRIDE):
        rows = xs[pl.ds(l, ns, stride=CMP_STRIDE), :]
        first = first + _dot((rows + pos_ref[0, 0, 0, l:l + 1, :]).astype(BF16), w1_ref[0, 0, 0, l])
        second = second + _dot((rows + pos_ref[0, 0, 1, l:l + 1, :]).astype(BF16), w1_ref[0, 0, 1, l])
    hid = _gelu_tanh(first + pltpu.roll(second, ns - 1, axis=0))
    out = _dot(hid.astype(BF16), w2_ref[0, 0])

    @pl.when(pl.program_id(1) < NSA_KV_HEADS)
    def _():
        o_ref[0, 0] = _partial_rope(_rms_rows(out) * kw_ref[0, 0:1, :], c_ref[...], lo_ref[...], hi_ref[...])

    @pl.when(pl.program_id(1) >= NSA_KV_HEADS)
    def _():
        o_ref[0, 0] = out


def _compress(proj, pos5, w1, w2, kw3, tables, layer, seq, batch, colkv):
    g = NSA_KV_HEADS
    n4 = 2 * g
    ns = seq // CMP_STRIDE
    half = CMP_LEN // CMP_STRIDE
    hidden = w1.shape[-1]
    full = pl.BlockSpec((ns, HEAD_DIM), lambda b, c: (0, 0))
    return pl.pallas_call(
        _compress_body,
        grid=(batch, n4),
        in_specs=[pl.BlockSpec((seq, HEAD_DIM), lambda b, c: (b, colkv + c)),
                  pl.BlockSpec((1, 1, half, CMP_STRIDE, HEAD_DIM), lambda b, c: (layer, c // g, 0, 0, 0)),
                  pl.BlockSpec((1, 1, half, CMP_STRIDE, HEAD_DIM, hidden),
                               lambda b, c: (layer, c // g, 0, 0, 0, 0)),
                  pl.BlockSpec((1, 1, hidden, HEAD_DIM), lambda b, c: (layer, c // g, 0, 0)),
                  pl.BlockSpec((1, 3, HEAD_DIM), lambda b, c: (layer, 0, 0)),
                  full, full, full],
        out_specs=pl.BlockSpec((1, 1, ns, HEAD_DIM), lambda b, c: (b, c, 0, 0)),
        out_shape=jax.ShapeDtypeStruct((batch, n4, ns, HEAD_DIM), F32),
        scratch_shapes=[pltpu.VMEM((seq, HEAD_DIM), F32)],
        compiler_params=_params("parallel", "parallel"),
        name="nsa_compress",
    )(proj, pos5, w1, w2, kw3, *tables)


def _cmp_select_body(q_ref, kc_ref, vc_ref, ov_ref, o_ref, sel_ref, score_t, *, rep, n_cmp, n_slc, top_k):
    i = pl.program_id(2)
    tq = q_ref.shape[0]
    ncp = kc_ref.shape[2]
    nsp = ov_ref.shape[1]
    kc = kc_ref[0, 0].astype(BF16)
    vc = vc_ref[0, 0].astype(BF16)
    tpos = i * tq + lax.broadcasted_iota(jnp.int32, (tq, ncp), 0)
    blk_n = lax.broadcasted_iota(jnp.int32, (tq, ncp), 1)
    visible = (blk_n * CMP_STRIDE + (CMP_LEN - 1) <= tpos) & (blk_n < n_cmp)
    p_sum = jnp.zeros((tq, ncp), F32)
    for r in range(rep):
        cols = slice(r * HEAD_DIM, (r + 1) * HEAD_DIM)
        s = jnp.where(visible, _dot_nt(q_ref[:, cols], kc), -jnp.inf)
        m = jnp.max(s, axis=-1, keepdims=True)
        m = jnp.where(m > -jnp.inf, m, 0.0)
        e = jnp.where(visible, jnp.exp(s - m), 0.0)
        p = e / jnp.maximum(jnp.sum(e, axis=-1, keepdims=True), 1e-30)
        o_ref[:, cols] = _dot(p.astype(BF16), vc)
        p_sum = p_sum + p
    p_slc = _dot_split(p_sum, ov_ref[...])

    tpos_s = i * tq + lax.broadcasted_iota(jnp.int32, (tq, nsp), 0)
    blk_s = lax.broadcasted_iota(jnp.int32, (tq, nsp), 1)
    cur = tpos_s // SLC_LEN
    forced = (blk_s == 0) | (blk_s == cur) | (blk_s == cur - 1)
    valid = blk_s <= cur
    score = jnp.where(forced, FORCED_SCORE, jnp.where(valid, p_slc, NEG_BIG))
    score_t[...] = score.T
    n_rows = _round_up(n_slc, SUBLANES)
    mine = score_t[0:n_rows, :]
    my_blk = lax.broadcasted_iota(jnp.int32, (n_rows, tq), 0)

    def count(other, rank):
        row = score_t[pl.ds(other, 1), :]
        strictly = jnp.where(row > mine, 1.0, 0.0)
        or_equal = jnp.where(row >= mine, 1.0, 0.0)
        return rank + jnp.where(my_blk > other, or_equal, strictly)

    rank = lax.fori_loop(0, n_slc, count, jnp.zeros((n_rows, tq), F32), unroll=4)
    score_t[0:n_rows, :] = jnp.where(rank < top_k, 1.0, 0.0)
    chosen = score_t[...].T
    sel_ref[0] = jnp.where(valid, chosen, 0.0).astype(sel_ref.dtype)


def _cmp_select(qn, cmp_kv, overlap, seq, batch, gw):
    t = qn.shape[0]
    g = NSA_KV_HEADS
    rep = gw // HEAD_DIM // g
    ncp = cmp_kv.shape[2]
    nsp = overlap.shape[1]
    n_slc = seq // SLC_LEN
    tq = 256
    nq = seq // tq
    body = functools.partial(_cmp_select_body, rep=rep, n_cmp=(seq - CMP_LEN) // CMP_STRIDE + 1,
                             n_slc=n_slc, top_k=min(SLC_TOPK, n_slc))
    return pl.pallas_call(
        body,
        grid=(batch, g, nq),
        in_specs=[pl.BlockSpec((tq, rep * HEAD_DIM), lambda b, gi, i: (b * nq + i, gi)),
                  pl.BlockSpec((1, 1, ncp, HEAD_DIM), lambda b, gi, i: (b, gi, 0, 0)),
                  pl.BlockSpec((1, 1, ncp, HEAD_DIM), lambda b, gi, i: (b, g + gi, 0, 0)),
                  pl.BlockSpec((ncp, nsp), lambda b, gi, i: (0, 0))],
        out_specs=[pl.BlockSpec((tq, rep * HEAD_DIM), lambda b, gi, i: (b * nq + i, gi)),
                   pl.BlockSpec((1, tq, nsp), lambda b, gi, i: (gi, b * nq + i, 0))],
        out_shape=[jax.ShapeDtypeStruct((t, gw), F32), jax.ShapeDtypeStruct((g, t, nsp), BF16)],
        scratch_shapes=[pltpu.VMEM((nsp, tq), F32)],
        compiler_params=_params("parallel", "parallel", "parallel"),
        name="nsa_cmp_select",
    )(qn, cmp_kv, cmp_kv, overlap)


def _attend_body(q_ref, ks_ref, vs_ref, kw_ref, vw_ref, sel_ref, ex_ref, oc_ref, gl_ref, o_ref,
                 s_ref, m_ref, l_ref, acc_ref, *, rep, groups, blk, span):
    i = pl.program_id(1)
    tq = q_ref.shape[0]
    rows = rep * tq
    folds = blk // LANES
    n_blocks = ((i + 1) * tq + blk - 1) // blk

    def stacked_q(g):
        return jnp.concatenate(
            [q_ref[:, (g * rep + r) * HEAD_DIM:(g * rep + r + 1) * HEAD_DIM] for r in range(rep)], axis=0)

    m_ref[...] = jnp.full(m_ref.shape, NEG_BIG, F32)
    l_ref[...] = jnp.zeros(l_ref.shape, F32)
    acc_ref[...] = jnp.zeros(acc_ref.shape, F32)
    tpos = i * tq + lax.broadcasted_iota(jnp.int32, (tq, blk), 0)
    kcol = lax.broadcasted_iota(jnp.int32, (tq, blk), 1)

    def scores(j, carry):
        start = pl.multiple_of(j * blk, blk)
        causal = start + kcol <= tpos
        for g in range(groups):
            cols = slice(g * HEAD_DIM, (g + 1) * HEAD_DIM)
            picked = _dot(sel_ref[g], ex_ref[j])
            bias = jnp.where((picked > 0.5) & causal, 0.0, NEG_BIG)
            s = _dot_nt(stacked_q(g), ks_ref[pl.ds(start, blk), cols]) + jnp.concatenate([bias] * rep, axis=0)
            s_ref[g, j] = s
            m = m_ref[g]
            for c in range(folds):
                m = jnp.maximum(m, s[:, c * LANES:(c + 1) * LANES])
            m_ref[g] = m
        return carry

    lax.fori_loop(0, n_blocks, scores, 0)

    for g in range(groups):
        m_ref[g] = jnp.broadcast_to(jnp.max(m_ref[g], axis=-1, keepdims=True), (rows, LANES))

    def weights(j, carry):
        start = pl.multiple_of(j * blk, blk)
        for g in range(groups):
            cols = slice(g * HEAD_DIM, (g + 1) * HEAD_DIM)
            p = jnp.exp(s_ref[g, j] - jnp.concatenate([m_ref[g]] * folds, axis=1))
            l = l_ref[g]
            for c in range(folds):
                l = l + p[:, c * LANES:(c + 1) * LANES]
            l_ref[g] = l
            acc_ref[g] += _dot(p.astype(BF16), vs_ref[pl.ds(start, blk), cols])
        return carry

    lax.fori_loop(0, n_blocks, weights, 0)

    wstart = pl.multiple_of(jnp.maximum(i * tq + tq - span, 0), tq)
    wpos = i * tq + (lax.broadcasted_iota(jnp.int32, (rep * tq, span), 0) & (tq - 1))
    dist = wpos - (wstart + lax.broadcasted_iota(jnp.int32, (rep * tq, span), 1))
    wbias = jnp.where((dist >= 0) & (dist < WINDOW), 0.0, NEG_BIG)
    gates = _sigmoid(gl_ref[...])

    def gate(head, branch):
        c = 3 * head + branch
        return gates[:, c:c + 1]

    for g in range(groups):
        cols = slice(g * HEAD_DIM, (g + 1) * HEAD_DIM)
        s = _dot_nt(stacked_q(g), kw_ref[pl.ds(wstart, span), cols]) + wbias
        e = jnp.exp(s - jnp.max(s, axis=-1, keepdims=True))
        o_win = _dot(e.astype(BF16), vw_ref[pl.ds(wstart, span), cols]) / jnp.sum(e, axis=-1, keepdims=True)
        o_slc = acc_ref[g] / jnp.sum(l_ref[g], axis=-1, keepdims=True)
        for r in range(rep):
            head = g * rep + r
            hc = slice(head * HEAD_DIM, (head + 1) * HEAD_DIM)
            rows = slice(r * tq, (r + 1) * tq)
            mix = gate(head, 0) * oc_ref[:, hc] + gate(head, 1) * o_slc[rows, :] + gate(head, 2) * o_win[rows, :]
            o_ref[:, hc] = mix.astype(o_ref.dtype)


def _attend(qn, kk, proj, sel, expand, o_cmp, gl, seq, batch, gw, colkv):
    t = qn.shape[0]
    g = NSA_KV_HEADS
    kvw = g * HEAD_DIM
    rep = gw // HEAD_DIM // g
    nsp = sel.shape[2]
    tq = 128
    blk = expand.shape[2]
    nq = seq // tq
    span = WINDOW + tq
    ck = colkv * HEAD_DIM // kvw
    row_spec = pl.BlockSpec((tq, gw), lambda b, i: (b * nq + i, 0))
    return pl.pallas_call(
        functools.partial(_attend_body, rep=rep, groups=g, blk=blk, span=span),
        grid=(batch, nq),
        in_specs=[row_spec,
                  pl.BlockSpec((seq, kvw), lambda b, i: (b, 0)),
                  pl.BlockSpec((seq, kvw), lambda b, i: (b, ck + 3)),
                  pl.BlockSpec((seq, kvw), lambda b, i: (b, 1)),
                  pl.BlockSpec((seq, kvw), lambda b, i: (b, ck + 5)),
                  pl.BlockSpec((g, tq, nsp), lambda b, i: (0, b * nq + i, 0)),
                  pl.BlockSpec((seq // blk, nsp, blk), lambda b, i: (0, 0, 0)),
                  row_spec,
                  pl.BlockSpec((tq, LANES), lambda b, i: (b * nq + i, 0))],
        out_specs=row_spec,
        out_shape=jax.ShapeDtypeStruct((t, gw), BF16),
        scratch_shapes=[pltpu.VMEM((g, seq // blk, rep * tq, blk), F32),
                        pltpu.VMEM((g, rep * tq, LANES), F32), pltpu.VMEM((g, rep * tq, LANES), F32),
                        pltpu.VMEM((g, rep * tq, HEAD_DIM), F32)],
        compiler_params=_params("parallel", "parallel"),
        name="nsa_attend",
    )(qn, kk, proj, kk, proj, sel, expand, o_cmp, gl)


def _nsa(proj, gl, qw3, kw3, pos4, w1, w2, layer, seq, batch, gw, colq, colkv):
    g = NSA_KV_HEADS
    kvw = g * HEAD_DIM
    tables = _rope_tables(jnp.arange(seq))
    qn, kk = _nsa_prep(proj, qw3, kw3, tables, layer, seq, batch, gw, colq, colkv)

    ns = seq // CMP_STRIDE
    n_cmp = (seq - CMP_LEN) // CMP_STRIDE + 1
    cmp_end = jnp.arange(ns) * CMP_STRIDE + CMP_LEN - 1
    cmp_kv = _compress(proj, pos4, w1, w2, kw3, _rope_tables(cmp_end), layer, seq, batch, colkv)

    n_slc = seq // SLC_LEN
    nsp = _round_up(n_slc, LANES)
    ci = np.arange(ns)[:, None] * CMP_STRIDE
    sj = np.arange(nsp)[None, :] * SLC_LEN
    ov = np.maximum(0, np.minimum(ci + CMP_LEN, sj + SLC_LEN) - np.maximum(ci, sj)) / CMP_STRIDE
    ov = np.where((np.arange(ns)[:, None] < n_cmp) & (np.arange(nsp)[None, :] < n_slc), ov, 0.0)
    o_cmp, sel = _cmp_select(qn, cmp_kv, jnp.asarray(ov, BF16), seq, batch, gw)

    blk = 512
    key_blk = (np.arange(seq) // SLC_LEN).reshape(seq // blk, 1, blk)
    expand = jnp.asarray(np.arange(nsp)[None, :, None] == key_blk, BF16)
    return _attend(qn, kk, proj, sel, expand, o_cmp, gl, seq, batch, gw, colkv)


def kernel(x, attn_norm_w, w_in, lru_conv_w, lru_conv_b, lru_w_a, lru_b_a, lru_w_x, lru_b_x, lru_lambda,
           nsa_q_norm_w, nsa_k_norm_w, nsa_cmp_pos, nsa_cmp_w1, nsa_cmp_w2, w_out, mlp_norm_w,
           w_gate, w_up, mlp_conv_w, mlp_conv_b, w_down):
    batch, seq, d = x.shape
    depth = w_in.shape[0]
    gw = d // N_MIXERS
    gh = gw // HEAD_DIM
    kvw = NSA_KV_HEADS * HEAD_DIM
    n_main = 10 * gw + 6 * kvw
    n_gate = w_in.shape[2] - n_main
    t = batch * seq
    assert seq % 512 == 0 and seq >= WINDOW + 128 and gw % (NSA_KV_HEADS * HEAD_DIM) == 0
    assert n_main % 512 == 0 and n_gate <= LANES and d % 1024 == 0 and w_gate.shape[2] % LANES == 0

    w_in_b = w_in.astype(BF16)
    w_out_b = w_out.astype(BF16)
    w_gate_b = w_gate.astype(BF16)
    w_up_b = w_up.astype(BF16)
    w_down_b = w_down.astype(BF16)
    mlp_cb3 = mlp_conv_b[:, None, :]
    attn_w3 = attn_norm_w[:, None, :]
    mlp_w3 = mlp_norm_w[:, None, :]
    lru_cb3, lru_ba3, lru_bx3, lru_lam3 = (v[:, None, :] for v in (lru_conv_b, lru_b_a, lru_b_x, lru_lambda))
    lru_wa_b = lru_w_a.astype(BF16)
    lru_wx_b = lru_w_x.astype(BF16)
    qw3 = nsa_q_norm_w[:, None, :]
    half = CMP_LEN // CMP_STRIDE
    pos4 = nsa_cmp_pos.reshape(depth, 2, half, CMP_STRIDE, HEAD_DIM)
    cmp_w1 = nsa_cmp_w1.reshape(depth, 2, half, CMP_STRIDE, HEAD_DIM, -1).astype(BF16)
    cmp_w2 = nsa_cmp_w2.astype(BF16)

    xf = x.reshape(t, d)
    for layer in range(depth):
        hn = _rmsnorm(xf, attn_w3, layer)
        proj = _matmul(hn, w_in_b, layer, 0, n_main, 1024, 512, BF16, "inproj")
        gl = _matmul(hn, w_in_b, layer, n_main // LANES, LANES, 1024, LANES, F32, "inproj_gates")
        y_ret = _retention(proj, seq, batch, gw)
        y_sb = _stickbreak(proj, seq, batch, gw, 4 * gh)
        y_lru = _rglru(proj, lru_conv_w, lru_cb3, lru_wa_b, lru_ba3, lru_wx_b, lru_bx3, lru_lam3,
                       layer, seq, batch, gw, 7 * gh)
        y_nsa = _nsa(proj, gl, qw3, nsa_k_norm_w, pos4, cmp_w1, cmp_w2, layer, seq, batch, gw,
                     9 * gh, 10 * gh)
        xf = _outproj((y_ret, y_sb, y_lru, y_nsa), w_out_b, xf, layer, 1024, 512)
        hn = _rmsnorm(xf, mlp_w3, layer)
        hid = _gateup(hn, w_gate_b, w_up_b, mlp_conv_w, mlp_cb3, layer, seq, 512, 1024)
        xf = _down(hid, w_down_b, xf, layer, 256, 1024)
    return xf.reshape(batch, seq, d)
```

```python
import functools
import math

import numpy as np
import jax
import jax.numpy as jnp
from jax import lax
from jax.experimental import pallas as pl
from jax.experimental.pallas import tpu as pltpu

F32 = jnp.float32
BF16 = jnp.bfloat16

HEAD_DIM = 128
N_MIXERS = 4
RET_CHUNK = 128
RET_ROT_BASE = 10000.0
LRU_CONV = 4
LRU_C = 8.0
NSA_KV_HEADS = 2
CMP_LEN = 32
CMP_STRIDE = 16
SLC_LEN = 64
SLC_TOPK = 16
WINDOW = 512
ROPE_THETA = 500000.0
ROPE_DIMS = HEAD_DIM // 4
MLP_CONV = 3
NORM_EPS = 1e-6

LANES = 128
SUBLANES = 8
VMEM_LIMIT_BYTES = 56 * 1024 * 1024
NEG_BIG = -1e30
FORCED_SCORE = 1e30


def _params(*sem, flags=None):
    return pltpu.CompilerParams(dimension_semantics=sem, vmem_limit_bytes=VMEM_LIMIT_BYTES, flags=flags)


def _round_up(x, m):
    return (x + m - 1) // m * m


def _dot(a, b):
    return jnp.dot(a, b, preferred_element_type=F32)


def _dot_nt(a, b):
    return lax.dot_general(a, b, (((1,), (1,)), ((), ())), preferred_element_type=F32)


def _dot_split(x, m):
    hi = x.astype(BF16)
    lo = (x - hi.astype(F32)).astype(BF16)
    return _dot(hi, m) + _dot(lo, m)


def _sigmoid(x):
    return 1.0 / (1.0 + jnp.exp(-x))


def _softplus(x):
    return jnp.maximum(x, 0.0) + jnp.log1p(jnp.exp(-jnp.abs(x)))


def _gelu_tanh(x):
    return x * (0.5 * (1.0 + jnp.tanh(math.sqrt(2.0 / math.pi) * (x + 0.044715 * (x * x * x)))))


def _rms_rows(x):
    return x * lax.rsqrt(jnp.mean(x * x, axis=-1, keepdims=True) + NORM_EPS)


def _rmsnorm_body(x_ref, w_ref, o_ref):
    o_ref[...] = (_rms_rows(x_ref[...]) * w_ref[0]).astype(o_ref.dtype)


def _rmsnorm(x, w3, layer):
    t, d = x.shape
    tr = 256
    return pl.pallas_call(
        _rmsnorm_body,
        grid=(t // tr,),
        in_specs=[pl.BlockSpec((tr, d), lambda i: (i, 0)),
                  pl.BlockSpec((1, 1, d), lambda i: (layer, 0, 0))],
        out_specs=pl.BlockSpec((tr, d), lambda i: (i, 0)),
        out_shape=jax.ShapeDtypeStruct((t, d), BF16),
        compiler_params=_params("parallel"),
        name="rmsnorm",
    )(x, w3)


def _matmul_body(a_ref, w_ref, o_ref):
    o_ref[...] = _dot(a_ref[...], w_ref[0]).astype(o_ref.dtype)


def _matmul(a, w, layer, col0, n, tm, tn, out_dtype, name):
    t, k = a.shape
    return pl.pallas_call(
        _matmul_body,
        grid=(t // tm, n // tn),
        in_specs=[pl.BlockSpec((tm, k), lambda i, j: (i, 0)),
                  pl.BlockSpec((1, k, tn), lambda i, j: (layer, 0, col0 + j))],
        out_specs=pl.BlockSpec((tm, tn), lambda i, j: (i, j)),
        out_shape=jax.ShapeDtypeStruct((t, n), out_dtype),
        compiler_params=_params("parallel", "parallel"),
        name=name,
    )(a, w)


def _outproj_body(y0_ref, y1_ref, y2_ref, y3_ref, w_ref, x_ref, o_ref):
    gw = y0_ref.shape[1]
    acc = x_ref[...]
    for n, y_ref in enumerate((y0_ref, y1_ref, y2_ref, y3_ref)):
        acc = acc + _dot(y_ref[...], w_ref[0, n * gw:(n + 1) * gw, :])
    o_ref[...] = acc


def _outproj(ys, w, x, layer, tm, tn):
    t, d = x.shape
    gw = ys[0].shape[1]
    y_spec = pl.BlockSpec((tm, gw), lambda i, j: (i, 0))
    return pl.pallas_call(
        _outproj_body,
        grid=(t // tm, d // tn),
        in_specs=[y_spec, y_spec, y_spec, y_spec,
                  pl.BlockSpec((1, N_MIXERS * gw, tn), lambda i, j: (layer, 0, j)),
                  pl.BlockSpec((tm, tn), lambda i, j: (i, j))],
        out_specs=pl.BlockSpec((tm, tn), lambda i, j: (i, j)),
        out_shape=jax.ShapeDtypeStruct((t, d), F32),
        compiler_params=_params("parallel", "parallel"),
        name="outproj",
    )(*ys, w, x)


def _gateup_body(a_ref, wg_ref, wu_ref, cw_ref, cb_ref, o_ref, gbuf, *, tiles_per_seq, sub):
    i = pl.program_id(1)
    tm = a_ref.shape[0]
    tn = o_ref.shape[1]

    @pl.when(i % tiles_per_seq == 0)
    def _():
        gbuf[0:SUBLANES, :] = jnp.zeros((SUBLANES, tn), F32)

    for c in range(tn // sub):
        cols = slice(c * sub, (c + 1) * sub)
        g = _dot(a_ref[...], wg_ref[0, :, cols])
        u = _dot(a_ref[...], wu_ref[0, :, cols])
        gbuf[SUBLANES:SUBLANES + tm, cols] = g
        cw = cw_ref[0, :, cols]
        gt = (cb_ref[0, :, cols] + cw[2:3, :] * g
              + cw[1:2, :] * gbuf[SUBLANES - 1:SUBLANES - 1 + tm, cols]
              + cw[0:1, :] * gbuf[SUBLANES - 2:SUBLANES - 2 + tm, cols])
        o_ref[:, cols] = (gt * _sigmoid(gt) * u).astype(o_ref.dtype)
        gbuf[0:SUBLANES, cols] = g[tm - SUBLANES:tm, :]


def _gateup(a, wg, wu, cw, cb3, layer, seq, tm, tn):
    t, k = a.shape
    f = wg.shape[2]
    w_spec = pl.BlockSpec((1, k, tn), lambda j, i: (layer, 0, j))
    return pl.pallas_call(
        functools.partial(_gateup_body, tiles_per_seq=seq // tm, sub=tn),
        grid=(pl.cdiv(f, tn), t // tm),
        in_specs=[pl.BlockSpec((tm, k), lambda j, i: (i, 0)), w_spec, w_spec,
                  pl.BlockSpec((1, MLP_CONV, tn), lambda j, i: (layer, 0, j)),
                  pl.BlockSpec((1, 1, tn), lambda j, i: (layer, 0, j))],
        out_specs=pl.BlockSpec((tm, tn), lambda j, i: (i, j)),
        out_shape=jax.ShapeDtypeStruct((t, f), BF16),
        scratch_shapes=[pltpu.VMEM((SUBLANES + tm, tn), F32)],
        compiler_params=_params("parallel", "arbitrary"),
        name="mlp_gateup",
    )(a, wg, wu, cw, cb3)


def _down_body(h_ref, w_ref, x_ref, o_ref):
    o_ref[...] = x_ref[...] + _dot(h_ref[...], w_ref[0])


def _down(h, w, x, layer, tm, tn):
    t, f = h.shape
    d = x.shape[1]
    return pl.pallas_call(
        _down_body,
        grid=(d // tn, t // tm),
        in_specs=[pl.BlockSpec((tm, f), lambda j, i: (i, 0)),
                  pl.BlockSpec((1, f, tn), lambda j, i: (layer, 0, j), pipeline_mode=pl.Buffered(1)),
                  pl.BlockSpec((tm, tn), lambda j, i: (i, j))],
        out_specs=pl.BlockSpec((tm, tn), lambda j, i: (i, j)),
        out_shape=jax.ShapeDtypeStruct((t, d), F32),
        compiler_params=_params("parallel", "parallel"),
        name="mlp_down",
    )(h, w, x)


def _retention_body(q_ref, k_ref, v_ref, g_ref, cos_ref, sin_ref, inner_ref, qd_ref, kd_ref, cd_ref,
                    o_ref, state, *, heads):
    @pl.when(pl.program_id(1) == 0)
    def _():
        state[...] = jnp.zeros(state.shape, F32)

    cos = cos_ref[...]
    sin = sin_ref[...]

    def rot(x):
        return x * cos + pltpu.roll(x, HEAD_DIM // 2, axis=1) * sin

    for h in range(heads):
        cols = slice(h * HEAD_DIM, (h + 1) * HEAD_DIM)
        q = rot(q_ref[:, cols].astype(F32))
        k = rot(k_ref[:, cols].astype(F32)) * (HEAD_DIM ** -0.5)
        vb = v_ref[:, cols]
        qb = q.astype(BF16)
        inner = _dot_nt(qb, k.astype(BF16)) * inner_ref[h]
        st = state[h]
        o = _dot(inner.astype(BF16), vb) + _dot(qb, st.astype(BF16)) * qd_ref[h]
        kd = (k * kd_ref[h]).T.astype(BF16)
        state[h] = st * cd_ref[h] + _dot(kd, vb)
        g = g_ref[:, cols].astype(F32)
        o_ref[:, cols] = (_rms_rows(o) * (g * _sigmoid(g))).astype(o_ref.dtype)


def _retention(proj, seq, batch, gw):
    t = proj.shape[0]
    heads = gw // HEAD_DIM
    c = RET_CHUNK
    nc = seq // c
    pos = jnp.arange(seq)
    inv = 1.0 / (RET_ROT_BASE ** jnp.linspace(0.0, 1.0, HEAD_DIM // 2, dtype=F32))
    ang = pos.astype(F32)[:, None] * inv[None, :]
    cos, sin = jnp.cos(ang), jnp.sin(ang)
    cos_t = jnp.concatenate([cos, cos], axis=1)
    sin_t = jnp.concatenate([-sin, sin], axis=1)
    log_g = jnp.log(1.0 - 2.0 ** (-5.0 - jnp.arange(heads, dtype=F32)))
    idx = jnp.arange(c, dtype=F32)
    rel = idx[:, None] - idx[None, :]
    inner_decay = jnp.where(rel >= 0, jnp.exp(log_g[:, None, None] * jnp.maximum(rel, 0.0)), 0.0)
    q_decay = jnp.broadcast_to(jnp.exp(log_g[:, None] * (idx + 1.0))[..., None], (heads, c, HEAD_DIM))
    k_decay = jnp.broadcast_to(jnp.exp(log_g[:, None] * (c - 1.0 - idx))[..., None], (heads, c, HEAD_DIM))
    chunk_decay = jnp.broadcast_to(jnp.exp(log_g * c)[:, None, None], (heads, HEAD_DIM, HEAD_DIM))

    def col(n):
        return pl.BlockSpec((c, gw), lambda b, i: (b * nc + i, n))

    rope_spec = pl.BlockSpec((c, HEAD_DIM), lambda b, i: (i, 0))
    const_spec = pl.BlockSpec((heads, c, HEAD_DIM), lambda b, i: (0, 0, 0))
    return pl.pallas_call(
        functools.partial(_retention_body, heads=heads),
        grid=(batch, nc),
        in_specs=[col(0), col(1), col(2), col(3), rope_spec, rope_spec,
                  const_spec, const_spec, const_spec, const_spec],
        out_specs=pl.BlockSpec((c, gw), lambda b, i: (b * nc + i, 0)),
        out_shape=jax.ShapeDtypeStruct((t, gw), BF16),
        scratch_shapes=[pltpu.VMEM((heads, HEAD_DIM, HEAD_DIM), F32)],
        compiler_params=_params("parallel", "arbitrary"),
        name="retention",
    )(proj, proj, proj, proj, cos_t, sin_t, inner_decay, q_decay, k_decay, chunk_decay)


def _stickbreak_body(q_ref, k_ref, v_ref, tri_ref, o_ref, acc_ref, later_ref, *, bk, heads):
    i = pl.program_id(2)
    tq = q_ref.shape[0]
    tri = tri_ref[...]
    log2e = 1.0 / math.log(2.0)
    acc_ref[...] = jnp.zeros(acc_ref.shape, F32)
    later_ref[...] = jnp.zeros(later_ref.shape, F32)
    sign_bit = jnp.uint32(0x80000000)

    def tile(j, row0):
        masked = row0 is not None
        r0 = row0 if masked else 0
        nr = tq - r0
        start = pl.multiple_of(j * bk, bk)
        if masked:
            qpos = i * tq + r0 + lax.broadcasted_iota(jnp.int32, (nr, bk), 0)
            past = start + lax.broadcasted_iota(jnp.int32, (nr, bk), 1) < qpos
        for h in range(heads):
            cols = slice(h * HEAD_DIM, (h + 1) * HEAD_DIM)
            y = _dot_nt(q_ref[r0:, cols], k_ref[pl.ds(start, bk), cols]) * (HEAD_DIM ** -0.5 * log2e)
            minus_abs = pltpu.bitcast(pltpu.bitcast(y, jnp.uint32) | sign_bit, F32)
            stay = jnp.maximum(y, 0.0) + jnp.log(1.0 + jnp.exp2(minus_abs)) * log2e
            if masked:
                stay = jnp.where(past, stay, 0.0)
            hi = stay.astype(BF16)
            both = _dot(hi, tri)
            later = later_ref[h, r0:, :]
            w = jnp.exp2(y - (both + later))
            if masked:
                w = jnp.where(past, w, 0.0)
            acc_ref[h, r0:, :] += _dot(w.astype(BF16), v_ref[pl.ds(start, bk), cols])
            later_ref[h, r0:, :] = later + jnp.sum(stay, axis=1, keepdims=True)

    n_diag = tq // bk
    for d in reversed(range(n_diag)):
        tile(i * n_diag + d, d * bk)

    def body(jj, c):
        tile(i * n_diag - 1 - jj, None)
        return c

    lax.fori_loop(0, i * n_diag, body, 0)
    for h in range(heads):
        o_ref[:, h * HEAD_DIM:(h + 1) * HEAD_DIM] = acc_ref[h].astype(o_ref.dtype)


def _stickbreak(proj, seq, batch, gw, col0):
    t = proj.shape[0]
    tq, bk = 512, 256
    nq = seq // tq
    hb = gw // HEAD_DIM
    wb = hb * HEAD_DIM
    c0 = col0 * HEAD_DIM // wb
    per = gw // wb
    ii = np.arange(bk)
    tri = jnp.asarray(ii[:, None] >= ii[None, :], BF16)
    kv_spec = lambda n: pl.BlockSpec((seq, wb), lambda b, h, i: (b, c0 + n * per + h))
    return pl.pallas_call(
        functools.partial(_stickbreak_body, bk=bk, heads=hb),
        grid=(batch, per, nq),
        in_specs=[pl.BlockSpec((tq, wb), lambda b, h, i: (b * nq + i, c0 + h)),
                  kv_spec(1), kv_spec(2),
                  pl.BlockSpec((bk, bk), lambda b, h, i: (0, 0))],
        out_specs=pl.BlockSpec((tq, wb), lambda b, h, i: (b * nq + i, h)),
        out_shape=jax.ShapeDtypeStruct((t, gw), BF16),
        scratch_shapes=[pltpu.VMEM((hb, tq, HEAD_DIM), F32), pltpu.VMEM((hb, tq, 1), F32)],
        compiler_params=_params("parallel", "parallel", "arbitrary"),
        name="stickbreak",
    )(proj, proj, proj, tri)


def _rglru_body(gate_ref, rec_ref, cw_ref, cb_ref, wa_ref, ba_ref, wx_ref, bx_ref, lam_ref, o_ref,
                xbuf, abuf, ubuf, hprev, *, blocks):
    i = pl.program_id(1)
    ts = rec_ref.shape[0]
    gw = rec_ref.shape[1]

    @pl.when(i == 0)
    def _():
        xbuf[0:SUBLANES, :] = jnp.zeros((SUBLANES, gw), F32)
        hprev[...] = jnp.zeros(hprev.shape, F32)

    x = rec_ref[...].astype(F32)
    xbuf[SUBLANES:SUBLANES + ts, :] = x
    cw = cw_ref[0]
    xr = cb_ref[0] + cw[3:4, :] * x
    for k in range(LRU_CONV - 1):
        off = SUBLANES - (LRU_CONV - 1) + k
        xr = xr + cw[k:k + 1, :] * xbuf[off:off + ts, :]
    xbuf[0:SUBLANES, :] = x[ts - SUBLANES:ts, :]

    log_sig_lam = -_softplus(-lam_ref[0])
    xrb = xr.astype(BF16)
    bw = gw // blocks
    first = (lax.broadcasted_iota(jnp.int32, (ts, bw), 0) == 0) & (i == 0)
    for n in range(blocks):
        cols = slice(n * bw, (n + 1) * bw)
        r = _sigmoid(_dot(xrb[:, cols], wa_ref[0, n]) + ba_ref[0, :, cols])
        gi = _sigmoid(_dot(xrb[:, cols], wx_ref[0, n]) + bx_ref[0, :, cols])
        log_a = LRU_C * r * log_sig_lam[:, cols]
        mult = jnp.where(first, 1.0, jnp.sqrt(jnp.maximum(1.0 - jnp.exp(2.0 * log_a), 0.0)))
        abuf[:, cols] = jnp.exp(log_a)
        ubuf[:, cols] = mult * (gi * xr[:, cols])

    rows = lax.broadcasted_iota(jnp.int32, (SUBLANES, gw), 0)

    def group(gidx, h):
        start = pl.multiple_of(gidx * SUBLANES, SUBLANES)
        a = abuf[pl.ds(start, SUBLANES), :]
        u = ubuf[pl.ds(start, SUBLANES), :]
        for s in (1, 2, 4):
            keep = rows >= s
            a_sh = jnp.where(keep, pltpu.roll(a, s, axis=0), 1.0)
            u_sh = jnp.where(keep, pltpu.roll(u, s, axis=0), 0.0)
            u = u + a * u_sh
            a = a * a_sh
        hs = u + a * h
        ubuf[pl.ds(start, SUBLANES), :] = hs
        return jnp.broadcast_to(hs[SUBLANES - 1:SUBLANES, :], (SUBLANES, gw))

    hprev[...] = lax.fori_loop(0, ts // SUBLANES, group, hprev[...])
    o_ref[...] = (ubuf[...] * _gelu_tanh(gate_ref[...].astype(F32))).astype(o_ref.dtype)


def _rglru(proj, cw, cb3, wa, ba3, wx, bx3, lam3, layer, seq, batch, gw, col0):
    t = proj.shape[0]
    blocks = gw // HEAD_DIM
    ts = 256
    ns = seq // ts
    cpb = col0 // blocks
    vec_spec = pl.BlockSpec((1, 1, gw), lambda b, i: (layer, 0, 0))
    w_spec = pl.BlockSpec((1, blocks, HEAD_DIM, HEAD_DIM), lambda b, i: (layer, 0, 0, 0))
    return pl.pallas_call(
        functools.partial(_rglru_body, blocks=blocks),
        grid=(batch, ns),
        in_specs=[pl.BlockSpec((ts, gw), lambda b, i: (b * ns + i, cpb)),
                  pl.BlockSpec((ts, gw), lambda b, i: (b * ns + i, cpb + 1)),
                  pl.BlockSpec((1, LRU_CONV, gw), lambda b, i: (layer, 0, 0)),
                  vec_spec, w_spec, vec_spec, w_spec, vec_spec, vec_spec],
        out_specs=pl.BlockSpec((ts, gw), lambda b, i: (b * ns + i, 0)),
        out_shape=jax.ShapeDtypeStruct((t, gw), BF16),
        scratch_shapes=[pltpu.VMEM((SUBLANES + ts, gw), F32), pltpu.VMEM((ts, gw), F32),
                        pltpu.VMEM((ts, gw), F32), pltpu.VMEM((SUBLANES, gw), F32)],
        compiler_params=_params("parallel", "arbitrary"),
        name="rglru",
    )(proj, proj, cw, cb3, wa, ba3, wx, bx3, lam3)


def _rope_tables(pos):
    half = ROPE_DIMS // 2
    inv = ROPE_THETA ** (-jnp.arange(half, dtype=F32) / half)
    ang = pos.astype(F32)[:, None] * inv[None, :]
    cos, sin = jnp.cos(ang), jnp.sin(ang)
    n = pos.shape[0]
    zeros = jnp.zeros((n, half), F32)
    rest = HEAD_DIM - 2 * half
    c = jnp.concatenate([cos, cos, jnp.ones((n, rest), F32)], axis=1)
    s_lo = jnp.concatenate([-sin, zeros, jnp.zeros((n, rest), F32)], axis=1)
    s_hi = jnp.concatenate([zeros, sin, jnp.zeros((n, rest), F32)], axis=1)
    return c, s_lo, s_hi


def _partial_rope(x, c, s_lo, s_hi):
    half = ROPE_DIMS // 2
    return x * c + pltpu.roll(x, HEAD_DIM - half, axis=1) * s_lo + pltpu.roll(x, half, axis=1) * s_hi


def _nsa_prep_body(q_ref, ks_ref, kw_ref, qw_ref, kwt_ref, c_ref, lo_ref, hi_ref, qo_ref, ko_ref,
                   *, heads, kv_heads):
    c, s_lo, s_hi = c_ref[...], lo_ref[...], hi_ref[...]

    def prep(x, w):
        return _partial_rope(_rms_rows(x) * w, c, s_lo, s_hi)

    qw = qw_ref[0]
    for h in range(heads):
        cols = slice(h * HEAD_DIM, (h + 1) * HEAD_DIM)
        qo_ref[:, cols] = (prep(q_ref[:, cols].astype(F32), qw) * (HEAD_DIM ** -0.5)).astype(qo_ref.dtype)
    for n, src in enumerate((ks_ref, kw_ref)):
        w = kwt_ref[0, n + 1:n + 2, :]
        for g in range(kv_heads):
            cols = slice(g * HEAD_DIM, (g + 1) * HEAD_DIM)
            dst = slice((n * kv_heads + g) * HEAD_DIM, (n * kv_heads + g + 1) * HEAD_DIM)
            ko_ref[:, dst] = prep(src[:, cols].astype(F32), w).astype(ko_ref.dtype)


def _nsa_prep(proj, qw3, kw3, tables, layer, seq, batch, gw, colq, colkv):
    t = proj.shape[0]
    heads = gw // HEAD_DIM
    kvw = NSA_KV_HEADS * HEAD_DIM
    ts = 256
    ns = seq // ts
    rope_spec = pl.BlockSpec((ts, HEAD_DIM), lambda b, i: (i, 0))
    cq = colq * HEAD_DIM // gw
    ck = colkv * HEAD_DIM // kvw
    return pl.pallas_call(
        functools.partial(_nsa_prep_body, heads=heads, kv_heads=NSA_KV_HEADS),
        grid=(batch, ns),
        in_specs=[pl.BlockSpec((ts, gw), lambda b, i: (b * ns + i, cq)),
                  pl.BlockSpec((ts, kvw), lambda b, i: (b * ns + i, ck + 2)),
                  pl.BlockSpec((ts, kvw), lambda b, i: (b * ns + i, ck + 4)),
                  pl.BlockSpec((1, 1, HEAD_DIM), lambda b, i: (layer, 0, 0)),
                  pl.BlockSpec((1, 3, HEAD_DIM), lambda b, i: (layer, 0, 0)),
                  rope_spec, rope_spec, rope_spec],
        out_specs=[pl.BlockSpec((ts, gw), lambda b, i: (b * ns + i, 0)),
                   pl.BlockSpec((ts, 2 * kvw), lambda b, i: (b * ns + i, 0))],
        out_shape=[jax.ShapeDtypeStruct((t, gw), BF16), jax.ShapeDtypeStruct((t, 2 * kvw), BF16)],
        compiler_params=_params("parallel", "parallel"),
        name="nsa_prep",
    )(proj, proj, proj, qw3, kw3, *tables)


def _compress_body(x_ref, pos_ref, w1_ref, w2_ref, kw_ref, c_ref, lo_ref, hi_ref, o_ref, xs):
    xs[...] = x_ref[...].astype(F32)
    ns = o_ref.shape[2]
    hidden = w2_ref.shape[2]
    first = jnp.zeros((ns, hidden), F32)
    second = jnp.zeros((ns, hidden), F32)
    for l in range(CMP_STRIDE):
        rows = xs[pl.ds(l, ns, stride=CMP_STRIDE), :]
        first = first + _dot((rows + pos_ref[0, 0, 0, l:l + 1, :]).astype(BF16), w1_ref[0, 0, 0, l])
        second = second + _dot((rows + pos_ref[0, 0, 1, l:l + 1, :]).astype(BF16), w1_ref[0, 0, 1, l])
    hid = _gelu_tanh(first + pltpu.roll(second, ns - 1, axis=0))
    out = _dot(hid.astype(BF16), w2_ref[0, 0])

    @pl.when(pl.program_id(1) < NSA_KV_HEADS)
    def _():
        o_ref[0, 0] = _partial_rope(_rms_rows(out) * kw_ref[0, 0:1, :], c_ref[...], lo_ref[...], hi_ref[...])

    @pl.when(pl.program_id(1) >= NSA_KV_HEADS)
    def _():
        o_ref[0, 0] = out


def _compress(proj, pos5, w1, w2, kw3, tables, layer, seq, batch, colkv):
    g = NSA_KV_HEADS
    n4 = 2 * g
    ns = seq // CMP_STRIDE
    half = CMP_LEN // CMP_STRIDE
    hidden = w1.shape[-1]
    full = pl.BlockSpec((ns, HEAD_DIM), lambda b, c: (0, 0))
    return pl.pallas_call(
        _compress_body,
        grid=(batch, n4),
        in_specs=[pl.BlockSpec((seq, HEAD_DIM), lambda b, c: (b, colkv + c)),
                  pl.BlockSpec((1, 1, half, CMP_STRIDE, HEAD_DIM), lambda b, c: (layer, c // g, 0, 0, 0)),
                  pl.BlockSpec((1, 1, half, CMP_STRIDE, HEAD_DIM, hidden),
                               lambda b, c: (layer, c // g, 0, 0, 0, 0)),
                  pl.BlockSpec((1, 1, hidden, HEAD_DIM), lambda b, c: (layer, c // g, 0, 0)),
                  pl.BlockSpec((1, 3, HEAD_DIM), lambda b, c: (layer, 0, 0)),
                  full, full, full],
        out_specs=pl.BlockSpec((1, 1, ns, HEAD_DIM), lambda b, c: (b, c, 0, 0)),
        out_shape=jax.ShapeDtypeStruct((batch, n4, ns, HEAD_DIM), F32),
        scratch_shapes=[pltpu.VMEM((seq, HEAD_DIM), F32)],
        compiler_params=_params("parallel", "parallel"),
        name="nsa_compress",
    )(proj, pos5, w1, w2, kw3, *tables)


def _cmp_select_body(q_ref, kc_ref, vc_ref, ov_ref, o_ref, sel_ref, score_t, *, rep, n_cmp, n_slc, top_k):
    i = pl.program_id(2)
    tq = q_ref.shape[0]
    ncp = kc_ref.shape[2]
    nsp = ov_ref.shape[1]
    kc = kc_ref[0, 0].astype(BF16)
    vc = vc_ref[0, 0].astype(BF16)
    tpos = i * tq + lax.broadcasted_iota(jnp.int32, (tq, ncp), 0)
    blk_n = lax.broadcasted_iota(jnp.int32, (tq, ncp), 1)
    visible = (blk_n * CMP_STRIDE + (CMP_LEN - 1) <= tpos) & (blk_n < n_cmp)
    p_sum = jnp.zeros((tq, ncp), F32)
    for r in range(rep):
        cols = slice(r * HEAD_DIM, (r + 1) * HEAD_DIM)
        s = jnp.where(visible, _dot_nt(q_ref[:, cols], kc), -jnp.inf)
        m = jnp.max(s, axis=-1, keepdims=True)
        m = jnp.where(m > -jnp.inf, m, 0.0)
        e = jnp.where(visible, jnp.exp(s - m), 0.0)
        p = e / jnp.maximum(jnp.sum(e, axis=-1, keepdims=True), 1e-30)
        o_ref[:, cols] = _dot(p.astype(BF16), vc)
        p_sum = p_sum + p
    p_slc = _dot_split(p_sum, ov_ref[...])

    tpos_s = i * tq + lax.broadcasted_iota(jnp.int32, (tq, nsp), 0)
    blk_s = lax.broadcasted_iota(jnp.int32, (tq, nsp), 1)
    cur = tpos_s // SLC_LEN
    forced = (blk_s == 0) | (blk_s == cur) | (blk_s == cur - 1)
    valid = blk_s <= cur
    score = jnp.where(forced, FORCED_SCORE, jnp.where(valid, p_slc, NEG_BIG))
    score_t[...] = score.T
    n_groups = _round_up(n_slc, SUBLANES) // SUBLANES
    mine = [score_t[v * SUBLANES:(v + 1) * SUBLANES, :] for v in range(n_groups)]
    rank = [jnp.zeros((SUBLANES, tq), F32) for _ in range(n_groups)]
    sub = lax.broadcasted_iota(jnp.int32, (SUBLANES, tq), 0)
    for other in range(n_slc):
        row = score_t[other:other + 1, :]
        for v in range(n_groups):
            if v < other // SUBLANES:
                ahead = row > mine[v]
            elif v > other // SUBLANES:
                ahead = row >= mine[v]
            else:
                ahead = (row > mine[v]) | ((row == mine[v]) & (sub > other % SUBLANES))
            rank[v] = rank[v] + jnp.where(ahead, 1.0, 0.0)
    for v in range(n_groups):
        score_t[v * SUBLANES:(v + 1) * SUBLANES, :] = jnp.where(rank[v] < top_k, 1.0, 0.0)
    chosen = score_t[...].T
    sel_ref[0] = jnp.where(valid, chosen, 0.0).astype(sel_ref.dtype)


def _cmp_select(qn, cmp_kv, overlap, seq, batch, gw):
    t = qn.shape[0]
    g = NSA_KV_HEADS
    rep = gw // HEAD_DIM // g
    ncp = cmp_kv.shape[2]
    nsp = overlap.shape[1]
    n_slc = seq // SLC_LEN
    tq = 256
    nq = seq // tq
    body = functools.partial(_cmp_select_body, rep=rep, n_cmp=(seq - CMP_LEN) // CMP_STRIDE + 1,
                             n_slc=n_slc, top_k=min(SLC_TOPK, n_slc))
    return pl.pallas_call(
        body,
        grid=(batch, g, nq),
        in_specs=[pl.BlockSpec((tq, rep * HEAD_DIM), lambda b, gi, i: (b * nq + i, gi)),
                  pl.BlockSpec((1, 1, ncp, HEAD_DIM), lambda b, gi, i: (b, gi, 0, 0)),
                  pl.BlockSpec((1, 1, ncp, HEAD_DIM), lambda b, gi, i: (b, g + gi, 0, 0)),
                  pl.BlockSpec((ncp, nsp), lambda b, gi, i: (0, 0))],
        out_specs=[pl.BlockSpec((tq, rep * HEAD_DIM), lambda b, gi, i: (b * nq + i, gi)),
                   pl.BlockSpec((1, tq, nsp), lambda b, gi, i: (gi, b * nq + i, 0))],
        out_shape=[jax.ShapeDtypeStruct((t, gw), F32), jax.ShapeDtypeStruct((g, t, nsp), BF16)],
        scratch_shapes=[pltpu.VMEM((nsp, tq), F32)],
        compiler_params=_params("parallel", "parallel", "parallel"),
        name="nsa_cmp_select",
    )(qn, cmp_kv, cmp_kv, overlap)


def _attend_body(q_ref, ks_ref, vs_ref, kw_ref, vw_ref, sel_ref, ex_ref, oc_ref, gl_ref, o_ref,
                 s_ref, m_ref, l_ref, acc_ref, *, rep, groups, blk, span):
    i = pl.program_id(1)
    tq = q_ref.shape[0]
    rows = rep * tq
    folds = blk // LANES
    n_blocks = ((i + 1) * tq + blk - 1) // blk

    def stacked_q(g):
        return jnp.concatenate(
            [q_ref[:, (g * rep + r) * HEAD_DIM:(g * rep + r + 1) * HEAD_DIM] for r in range(rep)], axis=0)

    m_ref[...] = jnp.full(m_ref.shape, NEG_BIG, F32)
    l_ref[...] = jnp.zeros(l_ref.shape, F32)
    acc_ref[...] = jnp.zeros(acc_ref.shape, F32)
    tpos = i * tq + lax.broadcasted_iota(jnp.int32, (tq, blk), 0)
    kcol = lax.broadcasted_iota(jnp.int32, (tq, blk), 1)

    def scores(j, carry):
        start = pl.multiple_of(j * blk, blk)
        causal = start + kcol <= tpos
        for g in range(groups):
            cols = slice(g * HEAD_DIM, (g + 1) * HEAD_DIM)
            picked = _dot(sel_ref[g], ex_ref[j])
            bias = jnp.where((picked > 0.5) & causal, 0.0, NEG_BIG)
            s = _dot_nt(stacked_q(g), ks_ref[pl.ds(start, blk), cols]) + jnp.concatenate([bias] * rep, axis=0)
            s_ref[g, j] = s
            m = m_ref[g]
            for c in range(folds):
                m = jnp.maximum(m, s[:, c * LANES:(c + 1) * LANES])
            m_ref[g] = m
        return carry

    lax.fori_loop(0, n_blocks, scores, 0)

    for g in range(groups):
        m_ref[g] = jnp.broadcast_to(jnp.max(m_ref[g], axis=-1, keepdims=True), (rows, LANES))

    def weights(j, carry):
        start = pl.multiple_of(j * blk, blk)
        for g in range(groups):
            cols = slice(g * HEAD_DIM, (g + 1) * HEAD_DIM)
            p = jnp.exp(s_ref[g, j] - jnp.concatenate([m_ref[g]] * folds, axis=1))
            l = l_ref[g]
            for c in range(folds):
                l = l + p[:, c * LANES:(c + 1) * LANES]
            l_ref[g] = l
            acc_ref[g] += _dot(p.astype(BF16), vs_ref[pl.ds(start, blk), cols])
        return carry

    lax.fori_loop(0, n_blocks, weights, 0)

    wstart = pl.multiple_of(jnp.maximum(i * tq + tq - span, 0), tq)
    wpos = i * tq + (lax.broadcasted_iota(jnp.int32, (rep * tq, span), 0) & (tq - 1))
    dist = wpos - (wstart + lax.broadcasted_iota(jnp.int32, (rep * tq, span), 1))
    wbias = jnp.where((dist >= 0) & (dist < WINDOW), 0.0, NEG_BIG)
    gates = _sigmoid(gl_ref[...])

    def gate(head, branch):
        c = 3 * head + branch
        return gates[:, c:c + 1]

    for g in range(groups):
        cols = slice(g * HEAD_DIM, (g + 1) * HEAD_DIM)
        s = _dot_nt(stacked_q(g), kw_ref[pl.ds(wstart, span), cols]) + wbias
        e = jnp.exp(s - jnp.max(s, axis=-1, keepdims=True))
        o_win = _dot(e.astype(BF16), vw_ref[pl.ds(wstart, span), cols]) / jnp.sum(e, axis=-1, keepdims=True)
        o_slc = acc_ref[g] / jnp.sum(l_ref[g], axis=-1, keepdims=True)
        for r in range(rep):
            head = g * rep + r
            hc = slice(head * HEAD_DIM, (head + 1) * HEAD_DIM)
            rows = slice(r * tq, (r + 1) * tq)
            mix = gate(head, 0) * oc_ref[:, hc] + gate(head, 1) * o_slc[rows, :] + gate(head, 2) * o_win[rows, :]
            o_ref[:, hc] = mix.astype(o_ref.dtype)


def _attend(qn, kk, proj, sel, expand, o_cmp, gl, seq, batch, gw, colkv):
    t = qn.shape[0]
    g = NSA_KV_HEADS
    kvw = g * HEAD_DIM
    rep = gw // HEAD_DIM // g
    nsp = sel.shape[2]
    tq = 128
    blk = expand.shape[2]
    nq = seq // tq
    span = WINDOW + tq
    ck = colkv * HEAD_DIM // kvw
    row_spec = pl.BlockSpec((tq, gw), lambda b, i: (b * nq + i, 0))
    return pl.pallas_call(
        functools.partial(_attend_body, rep=rep, groups=g, blk=blk, span=span),
        grid=(batch, nq),
        in_specs=[row_spec,
                  pl.BlockSpec((seq, kvw), lambda b, i: (b, 0)),
                  pl.BlockSpec((seq, kvw), lambda b, i: (b, ck + 3)),
                  pl.BlockSpec((seq, kvw), lambda b, i: (b, 1)),
                  pl.BlockSpec((seq, kvw), lambda b, i: (b, ck + 5)),
                  pl.BlockSpec((g, tq, nsp), lambda b, i: (0, b * nq + i, 0)),
                  pl.BlockSpec((seq // blk, nsp, blk), lambda b, i: (0, 0, 0)),
                  row_spec,
                  pl.BlockSpec((tq, LANES), lambda b, i: (b * nq + i, 0))],
        out_specs=row_spec,
        out_shape=jax.ShapeDtypeStruct((t, gw), BF16),
        scratch_shapes=[pltpu.VMEM((g, seq // blk, rep * tq, blk), F32),
                        pltpu.VMEM((g, rep * tq, LANES), F32), pltpu.VMEM((g, rep * tq, LANES), F32),
                        pltpu.VMEM((g, rep * tq, HEAD_DIM), F32)],
        compiler_params=_params("parallel", "parallel"),
        name="nsa_attend",
    )(qn, kk, proj, kk, proj, sel, expand, o_cmp, gl)


def _nsa(proj, gl, qw3, kw3, pos4, w1, w2, layer, seq, batch, gw, colq, colkv):
    g = NSA_KV_HEADS
    kvw = g * HEAD_DIM
    tables = _rope_tables(jnp.arange(seq))
    qn, kk = _nsa_prep(proj, qw3, kw3, tables, layer, seq, batch, gw, colq, colkv)

    ns = seq // CMP_STRIDE
    n_cmp = (seq - CMP_LEN) // CMP_STRIDE + 1
    cmp_end = jnp.arange(ns) * CMP_STRIDE + CMP_LEN - 1
    cmp_kv = _compress(proj, pos4, w1, w2, kw3, _rope_tables(cmp_end), layer, seq, batch, colkv)

    n_slc = seq // SLC_LEN
    nsp = _round_up(n_slc, LANES)
    ci = np.arange(ns)[:, None] * CMP_STRIDE
    sj = np.arange(nsp)[None, :] * SLC_LEN
    ov = np.maximum(0, np.minimum(ci + CMP_LEN, sj + SLC_LEN) - np.maximum(ci, sj)) / CMP_STRIDE
    ov = np.where((np.arange(ns)[:, None] < n_cmp) & (np.arange(nsp)[None, :] < n_slc), ov, 0.0)
    o_cmp, sel = _cmp_select(qn, cmp_kv, jnp.asarray(ov, BF16), seq, batch, gw)

    blk = 512
    key_blk = (np.arange(seq) // SLC_LEN).reshape(seq // blk, 1, blk)
    expand = jnp.asarray(np.arange(nsp)[None, :, None] == key_blk, BF16)
    return _attend(qn, kk, proj, sel, expand, o_cmp, gl, seq, batch, gw, colkv)


def kernel(x, attn_norm_w, w_in, lru_conv_w, lru_conv_b, lru_w_a, lru_b_a, lru_w_x, lru_b_x, lru_lambda,
           nsa_q_norm_w, nsa_k_norm_w, nsa_cmp_pos, nsa_cmp_w1, nsa_cmp_w2, w_out, mlp_norm_w,
           w_gate, w_up, mlp_conv_w, mlp_conv_b, w_down):
    batch, seq, d = x.shape
    depth = w_in.shape[0]
    gw = d // N_MIXERS
    gh = gw // HEAD_DIM
    kvw = NSA_KV_HEADS * HEAD_DIM
    n_main = 10 * gw + 6 * kvw
    n_gate = w_in.shape[2] - n_main
    t = batch * seq
    assert seq % 512 == 0 and seq >= WINDOW + 128 and gw % (NSA_KV_HEADS * HEAD_DIM) == 0
    assert n_main % 512 == 0 and n_gate <= LANES and d % 1024 == 0 and w_gate.shape[2] % LANES == 0

    w_in_b = w_in.astype(BF16)
    w_out_b = w_out.astype(BF16)
    w_gate_b = w_gate.astype(BF16)
    w_up_b = w_up.astype(BF16)
    w_down_b = w_down.astype(BF16)
    mlp_cb3 = mlp_conv_b[:, None, :]
    attn_w3 = attn_norm_w[:, None, :]
    mlp_w3 = mlp_norm_w[:, None, :]
    lru_cb3, lru_ba3, lru_bx3, lru_lam3 = (v[:, None, :] for v in (lru_conv_b, lru_b_a, lru_b_x, lru_lambda))
    lru_wa_b = lru_w_a.astype(BF16)
    lru_wx_b = lru_w_x.astype(BF16)
    qw3 = nsa_q_norm_w[:, None, :]
    half = CMP_LEN // CMP_STRIDE
    pos4 = nsa_cmp_pos.reshape(depth, 2, half, CMP_STRIDE, HEAD_DIM)
    cmp_w1 = nsa_cmp_w1.reshape(depth, 2, half, CMP_STRIDE, HEAD_DIM, -1).astype(BF16)
    cmp_w2 = nsa_cmp_w2.astype(BF16)

    xf = x.reshape(t, d)
    for layer in range(depth):
        hn = _rmsnorm(xf, attn_w3, layer)
        proj = _matmul(hn, w_in_b, layer, 0, n_main, 1024, 512, BF16, "inproj")
        gl = _matmul(hn, w_in_b, layer, n_main // LANES, LANES, 1024, LANES, F32, "inproj_gates")
        y_ret = _retention(proj, seq, batch, gw)
        y_sb = _stickbreak(proj, seq, batch, gw, 4 * gh)
        y_lru = _rglru(proj, lru_conv_w, lru_cb3, lru_wa_b, lru_ba3, lru_wx_b, lru_bx3, lru_lam3,
                       layer, seq, batch, gw, 7 * gh)
        y_nsa = _nsa(proj, gl, qw3, nsa_k_norm_w, pos4, cmp_w1, cmp_w2, layer, seq, batch, gw,
                     9 * gh, 10 * gh)
        xf = _outproj((y_ret, y_sb, y_lru, y_nsa), w_out_b, xf, layer, 1024, 512)
        hn = _rmsnorm(xf, mlp_w3, layer)
        hid = _gateup(hn, w_gate_b, w_up_b, mlp_conv_w, mlp_cb3, layer, seq, 512, 1024)
        xf = _down(hid, w_down_b, xf, layer, 256, 1024)
    return xf.reshape(batch, seq, d)
```

```python
import functools
import math

import numpy as np
import jax
import jax.numpy as jnp
from jax import lax
from jax.experimental import pallas as pl
from jax.experimental.pallas import tpu as pltpu

F32 = jnp.float32
BF16 = jnp.bfloat16

HEAD_DIM = 128
N_MIXERS = 4
RET_CHUNK = 128
RET_ROT_BASE = 10000.0
LRU_CONV = 4
LRU_C = 8.0
NSA_KV_HEADS = 2
CMP_LEN = 32
CMP_STRIDE = 16
SLC_LEN = 64
SLC_TOPK = 16
WINDOW = 512
ROPE_THETA = 500000.0
ROPE_DIMS = HEAD_DIM // 4
MLP_CONV = 3
NORM_EPS = 1e-6

LANES = 128
SUBLANES = 8
VMEM_LIMIT_BYTES = 56 * 1024 * 1024
NEG_BIG = -1e30
FORCED_SCORE = 1e30


def _params(*sem, flags=None):
    return pltpu.CompilerParams(dimension_semantics=sem, vmem_limit_bytes=VMEM_LIMIT_BYTES, flags=flags)


def _round_up(x, m):
    return (x + m - 1) // m * m


def _dot(a, b):
    return jnp.dot(a, b, preferred_element_type=F32)


def _dot_nt(a, b):
    return lax.dot_general(a, b, (((1,), (1,)), ((), ())), preferred_element_type=F32)


def _dot_split(x, m):
    hi = x.astype(BF16)
    lo = (x - hi.astype(F32)).astype(BF16)
    return _dot(hi, m) + _dot(lo, m)


def _sigmoid(x):
    return 1.0 / (1.0 + jnp.exp(-x))


def _softplus(x):
    return jnp.maximum(x, 0.0) + jnp.log1p(jnp.exp(-jnp.abs(x)))


def _gelu_tanh(x):
    return x * (0.5 * (1.0 + jnp.tanh(math.sqrt(2.0 / math.pi) * (x + 0.044715 * (x * x * x)))))


def _rms_rows(x):
    return x * lax.rsqrt(jnp.mean(x * x, axis=-1, keepdims=True) + NORM_EPS)


def _fold_lanes(v):
    out = v[:, 0:LANES]
    for c in range(1, v.shape[1] // LANES):
        out = out + v[:, c * LANES:(c + 1) * LANES]
    return out


def _inv_rms(folded, d):
    total = jnp.sum(folded, axis=-1, keepdims=True)
    return jnp.broadcast_to(lax.rsqrt(total * (1.0 / d) + NORM_EPS), folded.shape)


def _widen(r, n):
    return jnp.concatenate([r] * (n // LANES), axis=1)


def _row_stats_body(x_ref, xb_ref, p_ref):
    x = x_ref[...]
    xb_ref[...] = x.astype(BF16)
    p_ref[0] = _fold_lanes(x * x)


def _row_stats(x):
    t, d = x.shape
    tr = 256
    return pl.pallas_call(
        _row_stats_body,
        grid=(t // tr,),
        in_specs=[pl.BlockSpec((tr, d), lambda i: (i, 0))],
        out_specs=[pl.BlockSpec((tr, d), lambda i: (i, 0)), pl.BlockSpec((1, tr, LANES), lambda i: (0, i, 0))],
        out_shape=[jax.ShapeDtypeStruct((t, d), BF16), jax.ShapeDtypeStruct((1, t, LANES), F32)],
        compiler_params=_params("parallel"),
        name="row_stats",
    )(x)


def _inproj_body(a_ref, w_ref, p_ref, o_ref, r_ref, scaled, *, d):
    @pl.when(pl.program_id(1) == 0)
    def _():
        folded = p_ref[0]
        for n in range(1, p_ref.shape[0]):
            folded = folded + p_ref[n]
        r = _inv_rms(folded, d)
        r_ref[...] = r
        scaled[...] = (a_ref[...].astype(F32) * _widen(r, d)).astype(BF16)

    o_ref[...] = _dot(scaled[...], w_ref[0]).astype(o_ref.dtype)


def _inproj(a, w, parts, layer, n, tm, tn):
    t, k = a.shape
    return pl.pallas_call(
        functools.partial(_inproj_body, d=k),
        grid=(t // tm, n // tn),
        in_specs=[pl.BlockSpec((tm, k), lambda i, j: (i, 0)),
                  pl.BlockSpec((1, k, tn), lambda i, j: (layer, 0, j)),
                  pl.BlockSpec((parts.shape[0], tm, LANES), lambda i, j: (0, i, 0))],
        out_specs=[pl.BlockSpec((tm, tn), lambda i, j: (i, j)), pl.BlockSpec((tm, LANES), lambda i, j: (i, 0))],
        out_shape=[jax.ShapeDtypeStruct((t, n), BF16), jax.ShapeDtypeStruct((t, LANES), F32)],
        scratch_shapes=[pltpu.VMEM((tm, k), BF16)],
        compiler_params=_params("parallel", "arbitrary"),
        name="inproj",
    )(a, w, parts)


def _gate_logits_body(a_ref, w_ref, r_ref, o_ref):
    o_ref[...] = _dot(a_ref[...], w_ref[0]) * r_ref[...]


def _gate_logits(a, w, r, layer, col, tm):
    t, k = a.shape
    return pl.pallas_call(
        _gate_logits_body,
        grid=(t // tm,),
        in_specs=[pl.BlockSpec((tm, k), lambda i: (i, 0)),
                  pl.BlockSpec((1, k, LANES), lambda i: (layer, 0, col)),
                  pl.BlockSpec((tm, LANES), lambda i: (i, 0))],
        out_specs=pl.BlockSpec((tm, LANES), lambda i: (i, 0)),
        out_shape=jax.ShapeDtypeStruct((t, LANES), F32),
        compiler_params=_params("parallel"),
        name="inproj_gates",
    )(a, w, r)


def _outproj_body(y0_ref, y1_ref, y2_ref, y3_ref, w_ref, x_ref, o_ref, ob_ref, r_ref, ssq, *, d):
    j = pl.program_id(1)
    gw = y0_ref.shape[1]
    acc = x_ref[...]
    for n, y_ref in enumerate((y0_ref, y1_ref, y2_ref, y3_ref)):
        acc = acc + _dot(y_ref[...], w_ref[0, n * gw:(n + 1) * gw, :])
    o_ref[...] = acc
    ob_ref[...] = acc.astype(BF16)
    folded = _fold_lanes(acc * acc)

    @pl.when(j == 0)
    def _():
        ssq[...] = folded

    @pl.when(j > 0)
    def _():
        ssq[...] += folded

    @pl.when(j == pl.num_programs(1) - 1)
    def _():
        r_ref[...] = _inv_rms(ssq[...], d)


def _outproj(ys, w, x, layer, tm, tn):
    t, d = x.shape
    gw = ys[0].shape[1]
    y_spec = pl.BlockSpec((tm, gw), lambda i, j: (i, 0))
    tile = pl.BlockSpec((tm, tn), lambda i, j: (i, j))
    return pl.pallas_call(
        functools.partial(_outproj_body, d=d),
        grid=(t // tm, d // tn),
        in_specs=[y_spec, y_spec, y_spec, y_spec,
                  pl.BlockSpec((1, N_MIXERS * gw, tn), lambda i, j: (layer, 0, j)), tile],
        out_specs=[tile, tile, pl.BlockSpec((tm, LANES), lambda i, j: (i, 0))],
        out_shape=[jax.ShapeDtypeStruct((t, d), F32), jax.ShapeDtypeStruct((t, d), BF16),
                   jax.ShapeDtypeStruct((t, LANES), F32)],
        scratch_shapes=[pltpu.VMEM((tm, LANES), F32)],
        compiler_params=_params("parallel", "arbitrary"),
        name="outproj",
    )(*ys, w, x)


def _gateup_body(a_ref, r_ref, wg_ref, wu_ref, cw_ref, cb_ref, o_ref, gbuf, *, tiles_per_seq):
    i = pl.program_id(1)
    tm = a_ref.shape[0]
    tn = o_ref.shape[1]

    @pl.when(i % tiles_per_seq == 0)
    def _():
        gbuf[0:SUBLANES, :] = jnp.zeros((SUBLANES, tn), F32)

    r = _widen(r_ref[...], tn)
    g = _dot(a_ref[...], wg_ref[0]) * r
    u = _dot(a_ref[...], wu_ref[0]) * r
    gbuf[SUBLANES:SUBLANES + tm, :] = g
    cw = cw_ref[0]
    gt = (cb_ref[0] + cw[2:3, :] * g
          + cw[1:2, :] * gbuf[SUBLANES - 1:SUBLANES - 1 + tm, :]
          + cw[0:1, :] * gbuf[SUBLANES - 2:SUBLANES - 2 + tm, :])
    o_ref[...] = (gt * _sigmoid(gt) * u).astype(o_ref.dtype)
    gbuf[0:SUBLANES, :] = g[tm - SUBLANES:tm, :]


def _gateup(a, r, wg, wu, cw, cb3, layer, seq, tm, tn):
    t, k = a.shape
    f = wg.shape[2]
    w_spec = pl.BlockSpec((1, k, tn), lambda j, i: (layer, 0, j))
    return pl.pallas_call(
        functools.partial(_gateup_body, tiles_per_seq=seq // tm),
        grid=(pl.cdiv(f, tn), t // tm),
        in_specs=[pl.BlockSpec((tm, k), lambda j, i: (i, 0)),
                  pl.BlockSpec((tm, LANES), lambda j, i: (i, 0)), w_spec, w_spec,
                  pl.BlockSpec((1, MLP_CONV, tn), lambda j, i: (layer, 0, j)),
                  pl.BlockSpec((1, 1, tn), lambda j, i: (layer, 0, j))],
        out_specs=pl.BlockSpec((tm, tn), lambda j, i: (i, j)),
        out_shape=jax.ShapeDtypeStruct((t, f), BF16),
        scratch_shapes=[pltpu.VMEM((SUBLANES + tm, tn), F32)],
        compiler_params=_params("parallel", "arbitrary"),
        name="mlp_gateup",
    )(a, r, wg, wu, cw, cb3)


def _down_body(h_ref, w_ref, x_ref, o_ref, ob_ref, p_ref):
    acc = x_ref[...] + _dot(h_ref[...], w_ref[0])
    o_ref[...] = acc
    ob_ref[...] = acc.astype(BF16)
    p_ref[0] = _fold_lanes(acc * acc)


def _down(h, w, x, layer, tm, tn):
    t, f = h.shape
    d = x.shape[1]
    tile = pl.BlockSpec((tm, tn), lambda j, i: (i, j))
    return pl.pallas_call(
        _down_body,
        grid=(d // tn, t // tm),
        in_specs=[pl.BlockSpec((tm, f), lambda j, i: (i, 0)),
                  pl.BlockSpec((1, f, tn), lambda j, i: (layer, 0, j), pipeline_mode=pl.Buffered(1)),
                  tile],
        out_specs=[tile, tile, pl.BlockSpec((1, tm, LANES), lambda j, i: (j, i, 0))],
        out_shape=[jax.ShapeDtypeStruct((t, d), F32), jax.ShapeDtypeStruct((t, d), BF16),
                   jax.ShapeDtypeStruct((d // tn, t, LANES), F32)],
        compiler_params=_params("parallel", "parallel"),
        name="mlp_down",
    )(h, w, x)


def _retention_body(q_ref, k_ref, v_ref, g_ref, cos_ref, sin_ref, inner_ref, qd_ref, kd_ref, cd_ref,
                    o_ref, state, *, heads):
    @pl.when(pl.program_id(1) == 0)
    def _():
        state[...] = jnp.zeros(state.shape, F32)

    cos = cos_ref[...]
    sin = sin_ref[...]

    def rot(x):
        return x * cos + pltpu.roll(x, HEAD_DIM // 2, axis=1) * sin

    for h in range(heads):
        cols = slice(h * HEAD_DIM, (h + 1) * HEAD_DIM)
        q = rot(q_ref[:, cols].astype(F32))
        k = rot(k_ref[:, cols].astype(F32)) * (HEAD_DIM ** -0.5)
        vb = v_ref[:, cols]
        qb = q.astype(BF16)
        inner = _dot_nt(qb, k.astype(BF16)) * inner_ref[h]
        st = state[h]
        o = _dot(inner.astype(BF16), vb) + _dot(qb, st.astype(BF16)) * qd_ref[h]
        kd = (k * kd_ref[h]).T.astype(BF16)
        state[h] = st * cd_ref[h] + _dot(kd, vb)
        g = g_ref[:, cols].astype(F32)
        o_ref[:, cols] = (_rms_rows(o) * (g * _sigmoid(g))).astype(o_ref.dtype)


def _retention(proj, seq, batch, gw):
    t = proj.shape[0]
    heads = gw // HEAD_DIM
    c = RET_CHUNK
    nc = seq // c
    pos = jnp.arange(seq)
    inv = 1.0 / (RET_ROT_BASE ** jnp.linspace(0.0, 1.0, HEAD_DIM // 2, dtype=F32))
    ang = pos.astype(F32)[:, None] * inv[None, :]
    cos, sin = jnp.cos(ang), jnp.sin(ang)
    cos_t = jnp.concatenate([cos, cos], axis=1)
    sin_t = jnp.concatenate([-sin, sin], axis=1)
    log_g = jnp.log(1.0 - 2.0 ** (-5.0 - jnp.arange(heads, dtype=F32)))
    idx = jnp.arange(c, dtype=F32)
    rel = idx[:, None] - idx[None, :]
    inner_decay = jnp.where(rel >= 0, jnp.exp(log_g[:, None, None] * jnp.maximum(rel, 0.0)), 0.0)
    q_decay = jnp.broadcast_to(jnp.exp(log_g[:, None] * (idx + 1.0))[..., None], (heads, c, HEAD_DIM))
    k_decay = jnp.broadcast_to(jnp.exp(log_g[:, None] * (c - 1.0 - idx))[..., None], (heads, c, HEAD_DIM))
    chunk_decay = jnp.broadcast_to(jnp.exp(log_g * c)[:, None, None], (heads, HEAD_DIM, HEAD_DIM))

    def col(n):
        return pl.BlockSpec((c, gw), lambda b, i: (b * nc + i, n))

    rope_spec = pl.BlockSpec((c, HEAD_DIM), lambda b, i: (i, 0))
    const_spec = pl.BlockSpec((heads, c, HEAD_DIM), lambda b, i: (0, 0, 0))
    return pl.pallas_call(
        functools.partial(_retention_body, heads=heads),
        grid=(batch, nc),
        in_specs=[col(0), col(1), col(2), col(3), rope_spec, rope_spec,
                  const_spec, const_spec, const_spec, const_spec],
        out_specs=pl.BlockSpec((c, gw), lambda b, i: (b * nc + i, 0)),
        out_shape=jax.ShapeDtypeStruct((t, gw), BF16),
        scratch_shapes=[pltpu.VMEM((heads, HEAD_DIM, HEAD_DIM), F32)],
        compiler_params=_params("parallel", "arbitrary"),
        name="retention",
    )(proj, proj, proj, proj, cos_t, sin_t, inner_decay, q_decay, k_decay, chunk_decay)


def _stickbreak_body(q_ref, k_ref, v_ref, tri_ref, o_ref, acc_ref, later_ref, *, bk, heads):
    i = pl.program_id(2)
    tq = q_ref.shape[0]
    tri = tri_ref[...]
    log2e = 1.0 / math.log(2.0)
    acc_ref[...] = jnp.zeros(acc_ref.shape, F32)
    later_ref[...] = jnp.zeros(later_ref.shape, F32)
    sign_bit = jnp.uint32(0x80000000)

    def tile(j, row0):
        masked = row0 is not None
        r0 = row0 if masked else 0
        nr = tq - r0
        start = pl.multiple_of(j * bk, bk)
        if masked:
            qpos = i * tq + r0 + lax.broadcasted_iota(jnp.int32, (nr, bk), 0)
            past = start + lax.broadcasted_iota(jnp.int32, (nr, bk), 1) < qpos
        for h in range(heads):
            cols = slice(h * HEAD_DIM, (h + 1) * HEAD_DIM)
            y = _dot_nt(q_ref[r0:, cols], k_ref[pl.ds(start, bk), cols]) * (HEAD_DIM ** -0.5 * log2e)
            minus_abs = pltpu.bitcast(pltpu.bitcast(y, jnp.uint32) | sign_bit, F32)
            stay = jnp.maximum(y, 0.0) + jnp.log(1.0 + jnp.exp2(minus_abs)) * log2e
            if masked:
                stay = jnp.where(past, stay, 0.0)
            hi = stay.astype(BF16)
            both = _dot(hi, tri)
            later = later_ref[h, r0:, :]
            w = jnp.exp2(y - (both + later))
            if masked:
                w = jnp.where(past, w, 0.0)
            acc_ref[h, r0:, :] += _dot(w.astype(BF16), v_ref[pl.ds(start, bk), cols])
            later_ref[h, r0:, :] = later + jnp.sum(stay, axis=1, keepdims=True)

    n_diag = tq // bk
    for d in reversed(range(n_diag)):
        tile(i * n_diag + d, d * bk)

    def body(jj, c):
        tile(i * n_diag - 1 - jj, None)
        return c

    lax.fori_loop(0, i * n_diag, body, 0)
    for h in range(heads):
        o_ref[:, h * HEAD_DIM:(h + 1) * HEAD_DIM] = acc_ref[h].astype(o_ref.dtype)


def _stickbreak(proj, seq, batch, gw, col0):
    t = proj.shape[0]
    tq, bk = 512, 256
    nq = seq // tq
    hb = gw // HEAD_DIM
    wb = hb * HEAD_DIM
    c0 = col0 * HEAD_DIM // wb
    per = gw // wb
    ii = np.arange(bk)
    tri = jnp.asarray(ii[:, None] >= ii[None, :], BF16)
    kv_spec = lambda n: pl.BlockSpec((seq, wb), lambda b, h, i: (b, c0 + n * per + h))
    return pl.pallas_call(
        functools.partial(_stickbreak_body, bk=bk, heads=hb),
        grid=(batch, per, nq),
        in_specs=[pl.BlockSpec((tq, wb), lambda b, h, i: (b * nq + i, c0 + h)),
                  kv_spec(1), kv_spec(2),
                  pl.BlockSpec((bk, bk), lambda b, h, i: (0, 0))],
        out_specs=pl.BlockSpec((tq, wb), lambda b, h, i: (b * nq + i, h)),
        out_shape=jax.ShapeDtypeStruct((t, gw), BF16),
        scratch_shapes=[pltpu.VMEM((hb, tq, HEAD_DIM), F32), pltpu.VMEM((hb, tq, 1), F32)],
        compiler_params=_params("parallel", "parallel", "arbitrary"),
        name="stickbreak",
    )(proj, proj, proj, tri)


def _rglru_body(gate_ref, rec_ref, cw_ref, cb_ref, wa_ref, ba_ref, wx_ref, bx_ref, lam_ref, o_ref,
                xbuf, abuf, ubuf, hprev, *, blocks):
    i = pl.program_id(1)
    ts = rec_ref.shape[0]
    gw = rec_ref.shape[1]

    @pl.when(i == 0)
    def _():
        xbuf[0:SUBLANES, :] = jnp.zeros((SUBLANES, gw), F32)
        hprev[...] = jnp.zeros(hprev.shape, F32)

    x = rec_ref[...].astype(F32)
    xbuf[SUBLANES:SUBLANES + ts, :] = x
    cw = cw_ref[0]
    xr = cb_ref[0] + cw[3:4, :] * x
    for k in range(LRU_CONV - 1):
        off = SUBLANES - (LRU_CONV - 1) + k
        xr = xr + cw[k:k + 1, :] * xbuf[off:off + ts, :]
    xbuf[0:SUBLANES, :] = x[ts - SUBLANES:ts, :]

    log_sig_lam = -_softplus(-lam_ref[0])
    xrb = xr.astype(BF16)
    bw = gw // blocks
    first = (lax.broadcasted_iota(jnp.int32, (ts, bw), 0) == 0) & (i == 0)
    for n in range(blocks):
        cols = slice(n * bw, (n + 1) * bw)
        r = _sigmoid(_dot(xrb[:, cols], wa_ref[0, n]) + ba_ref[0, :, cols])
        gi = _sigmoid(_dot(xrb[:, cols], wx_ref[0, n]) + bx_ref[0, :, cols])
        log_a = LRU_C * r * log_sig_lam[:, cols]
        mult = jnp.where(first, 1.0, jnp.sqrt(jnp.maximum(1.0 - jnp.exp(2.0 * log_a), 0.0)))
        abuf[:, cols] = jnp.exp(log_a)
        ubuf[:, cols] = mult * (gi * xr[:, cols])

    rows = lax.broadcasted_iota(jnp.int32, (SUBLANES, gw), 0)

    def group(gidx, h):
        start = pl.multiple_of(gidx * SUBLANES, SUBLANES)
        a = abuf[pl.ds(start, SUBLANES), :]
        u = ubuf[pl.ds(start, SUBLANES), :]
        for s in (1, 2, 4):
            keep = rows >= s
            a_sh = jnp.where(keep, pltpu.roll(a, s, axis=0), 1.0)
            u_sh = jnp.where(keep, pltpu.roll(u, s, axis=0), 0.0)
            u = u + a * u_sh
            a = a * a_sh
        hs = u + a * h
        ubuf[pl.ds(start, SUBLANES), :] = hs
        return jnp.broadcast_to(hs[SUBLANES - 1:SUBLANES, :], (SUBLANES, gw))

    hprev[...] = lax.fori_loop(0, ts // SUBLANES, group, hprev[...])
    o_ref[...] = (ubuf[...] * _gelu_tanh(gate_ref[...].astype(F32))).astype(o_ref.dtype)


def _rglru(proj, cw, cb3, wa, ba3, wx, bx3, lam3, layer, seq, batch, gw, col0):
    t = proj.shape[0]
    blocks = gw // HEAD_DIM
    ts = 256
    ns = seq // ts
    cpb = col0 // blocks
    vec_spec = pl.BlockSpec((1, 1, gw), lambda b, i: (layer, 0, 0))
    w_spec = pl.BlockSpec((1, blocks, HEAD_DIM, HEAD_DIM), lambda b, i: (layer, 0, 0, 0))
    return pl.pallas_call(
        functools.partial(_rglru_body, blocks=blocks),
        grid=(batch, ns),
        in_specs=[pl.BlockSpec((ts, gw), lambda b, i: (b * ns + i, cpb)),
                  pl.BlockSpec((ts, gw), lambda b, i: (b * ns + i, cpb + 1)),
                  pl.BlockSpec((1, LRU_CONV, gw), lambda b, i: (layer, 0, 0)),
                  vec_spec, w_spec, vec_spec, w_spec, vec_spec, vec_spec],
        out_specs=pl.BlockSpec((ts, gw), lambda b, i: (b * ns + i, 0)),
        out_shape=jax.ShapeDtypeStruct((t, gw), BF16),
        scratch_shapes=[pltpu.VMEM((SUBLANES + ts, gw), F32), pltpu.VMEM((ts, gw), F32),
                        pltpu.VMEM((ts, gw), F32), pltpu.VMEM((SUBLANES, gw), F32)],
        compiler_params=_params("parallel", "arbitrary"),
        name="rglru",
    )(proj, proj, cw, cb3, wa, ba3, wx, bx3, lam3)


def _rope_tables(pos):
    half = ROPE_DIMS // 2
    inv = ROPE_THETA ** (-jnp.arange(half, dtype=F32) / half)
    ang = pos.astype(F32)[:, None] * inv[None, :]
    cos, sin = jnp.cos(ang), jnp.sin(ang)
    n = pos.shape[0]
    zeros = jnp.zeros((n, half), F32)
    rest = HEAD_DIM - 2 * half
    c = jnp.concatenate([cos, cos, jnp.ones((n, rest), F32)], axis=1)
    s_lo = jnp.concatenate([-sin, zeros, jnp.zeros((n, rest), F32)], axis=1)
    s_hi = jnp.concatenate([zeros, sin, jnp.zeros((n, rest), F32)], axis=1)
    return c, s_lo, s_hi


def _partial_rope(x, c, s_lo, s_hi):
    half = ROPE_DIMS // 2
    return x * c + pltpu.roll(x, HEAD_DIM - half, axis=1) * s_lo + pltpu.roll(x, half, axis=1) * s_hi


def _nsa_prep_body(q_ref, ks_ref, kw_ref, qw_ref, kwt_ref, c_ref, lo_ref, hi_ref, qo_ref, ko_ref,
                   *, heads, kv_heads):
    c, s_lo, s_hi = c_ref[...], lo_ref[...], hi_ref[...]

    def prep(x, w):
        return _partial_rope(_rms_rows(x) * w, c, s_lo, s_hi)

    qw = qw_ref[0]
    for h in range(heads):
        cols = slice(h * HEAD_DIM, (h + 1) * HEAD_DIM)
        qo_ref[:, cols] = (prep(q_ref[:, cols].astype(F32), qw) * (HEAD_DIM ** -0.5)).astype(qo_ref.dtype)
    for n, src in enumerate((ks_ref, kw_ref)):
        w = kwt_ref[0, n + 1:n + 2, :]
        for g in range(kv_heads):
            cols = slice(g * HEAD_DIM, (g + 1) * HEAD_DIM)
            dst = slice((n * kv_heads + g) * HEAD_DIM, (n * kv_heads + g + 1) * HEAD_DIM)
            ko_ref[:, dst] = prep(src[:, cols].astype(F32), w).astype(ko_ref.dtype)


def _nsa_prep(proj, qw3, kw3, tables, layer, seq, batch, gw, colq, colkv):
    t = proj.shape[0]
    heads = gw // HEAD_DIM
    kvw = NSA_KV_HEADS * HEAD_DIM
    ts = 256
    ns = seq // ts
    rope_spec = pl.BlockSpec((ts, HEAD_DIM), lambda b, i: (i, 0))
    cq = colq * HEAD_DIM // gw
    ck = colkv * HEAD_DIM // kvw
    return pl.pallas_call(
        functools.partial(_nsa_prep_body, heads=heads, kv_heads=NSA_KV_HEADS),
        grid=(batch, ns),
        in_specs=[pl.BlockSpec((ts, gw), lambda b, i: (b * ns + i, cq)),
                  pl.BlockSpec((ts, kvw), lambda b, i: (b * ns + i, ck + 2)),
                  pl.BlockSpec((ts, kvw), lambda b, i: (b * ns + i, ck + 4)),
                  pl.BlockSpec((1, 1, HEAD_DIM), lambda b, i: (layer, 0, 0)),
                  pl.BlockSpec((1, 3, HEAD_DIM), lambda b, i: (layer, 0, 0)),
                  rope_spec, rope_spec, rope_spec],
        out_specs=[pl.BlockSpec((ts, gw), lambda b, i: (b * ns + i, 0)),
                   pl.BlockSpec((ts, 2 * kvw), lambda b, i: (b * ns + i, 0))],
        out_shape=[jax.ShapeDtypeStruct((t, gw), BF16), jax.ShapeDtypeStruct((t, 2 * kvw), BF16)],
        compiler_params=_params("parallel", "parallel"),
        name="nsa_prep",
    )(proj, proj, proj, qw3, kw3, *tables)


def _compress_body(x_ref, pos_ref, w1_ref, w2_ref, kw_ref, c_ref, lo_ref, hi_ref, o_ref, xs):
    xs[...] = x_ref[...].astype(F32)
    ns = o_ref.shape[2]
    hidden = w2_ref.shape[2]
    first = jnp.zeros((ns, hidden), F32)
    second = jnp.zeros((ns, hidden), F32)
    for l in range(CMP_STRIDE):
        rows = xs[pl.ds(l, ns, stride=CMP_STRIDE), :]
        first = first + _dot((rows + pos_ref[0, 0, 0, l:l + 1, :]).astype(BF16), w1_ref[0, 0, 0, l])
        second = second + _dot((rows + pos_ref[0, 0, 1, l:l + 1, :]).astype(BF16), w1_ref[0, 0, 1, l])
    hid = _gelu_tanh(first + pltpu.roll(second, ns - 1, axis=0))
    out = _dot(hid.astype(BF16), w2_ref[0, 0])

    @pl.when(pl.program_id(1) < NSA_KV_HEADS)
    def _():
        o_ref[0, 0] = _partial_rope(_rms_rows(out) * kw_ref[0, 0:1, :], c_ref[...], lo_ref[...], hi_ref[...])

    @pl.when(pl.program_id(1) >= NSA_KV_HEADS)
    def _():
        o_ref[0, 0] = out


def _compress(proj, pos5, w1, w2, kw3, tables, layer, seq, batch, colkv):
    g = NSA_KV_HEADS
    n4 = 2 * g
    ns = seq // CMP_STRIDE
    half = CMP_LEN // CMP_STRIDE
    hidden = w1.shape[-1]
    full = pl.BlockSpec((ns, HEAD_DIM), lambda b, c: (0, 0))
    return pl.pallas_call(
        _compress_body,
        grid=(batch, n4),
        in_specs=[pl.BlockSpec((seq, HEAD_DIM), lambda b, c: (b, colkv + c)),
                  pl.BlockSpec((1, 1, half, CMP_STRIDE, HEAD_DIM), lambda b, c: (layer, c // g, 0, 0, 0)),
                  pl.BlockSpec((1, 1, half, CMP_STRIDE, HEAD_DIM, hidden),
                               lambda b, c: (layer, c // g, 0, 0, 0, 0)),
                  pl.BlockSpec((1, 1, hidden, HEAD_DIM), lambda b, c: (layer, c // g, 0, 0)),
                  pl.BlockSpec((1, 3, HEAD_DIM), lambda b, c: (layer, 0, 0)),
                  full, full, full],
        out_specs=pl.BlockSpec((1, 1, ns, HEAD_DIM), lambda b, c: (b, c, 0, 0)),
        out_shape=jax.ShapeDtypeStruct((batch, n4, ns, HEAD_DIM), F32),
        scratch_shapes=[pltpu.VMEM((seq, HEAD_DIM), F32)],
        compiler_params=_params("parallel", "parallel"),
        name="nsa_compress",
    )(proj, pos5, w1, w2, kw3, *tables)


def _cmp_select_body(q_ref, kc_ref, vc_ref, ov_ref, o_ref, sel_ref, score_t, *, rep, n_cmp, n_slc, top_k):
    i = pl.program_id(2)
    tq = q_ref.shape[0]
    ncp = kc_ref.shape[2]
    nsp = ov_ref.shape[1]
    kc = kc_ref[0, 0].astype(BF16)
    vc = vc_ref[0, 0].astype(BF16)
    tpos = i * tq + lax.broadcasted_iota(jnp.int32, (tq, ncp), 0)
    blk_n = lax.broadcasted_iota(jnp.int32, (tq, ncp), 1)
    visible = (blk_n * CMP_STRIDE + (CMP_LEN - 1) <= tpos) & (blk_n < n_cmp)
    p_sum = jnp.zeros((tq, ncp), F32)
    for r in range(rep):
        cols = slice(r * HEAD_DIM, (r + 1) * HEAD_DIM)
        s = jnp.where(visible, _dot_nt(q_ref[:, cols], kc), -jnp.inf)
        m = jnp.max(s, axis=-1, keepdims=True)
        m = jnp.where(m > -jnp.inf, m, 0.0)
        e = jnp.where(visible, jnp.exp(s - m), 0.0)
        p = e / jnp.maximum(jnp.sum(e, axis=-1, keepdims=True), 1e-30)
        o_ref[:, cols] = _dot(p.astype(BF16), vc)
        p_sum = p_sum + p
    p_slc = _dot_split(p_sum, ov_ref[...])

    tpos_s = i * tq + lax.broadcasted_iota(jnp.int32, (tq, nsp), 0)
    blk_s = lax.broadcasted_iota(jnp.int32, (tq, nsp), 1)
    cur = tpos_s // SLC_LEN
    forced = (blk_s == 0) | (blk_s == cur) | (blk_s == cur - 1)
    valid = blk_s <= cur
    score = jnp.where(forced, FORCED_SCORE, jnp.where(valid, p_slc, NEG_BIG))
    score_t[...] = score.T
    n_groups = _round_up(n_slc, SUBLANES) // SUBLANES
    mine = [score_t[v * SUBLANES:(v + 1) * SUBLANES, :] for v in range(n_groups)]
    rank = [jnp.zeros((SUBLANES, tq), F32) for _ in range(n_groups)]
    sub = lax.broadcasted_iota(jnp.int32, (SUBLANES, tq), 0)
    for other in range(n_slc):
        row = score_t[other:other + 1, :]
        for v in range(n_groups):
            if v < other // SUBLANES:
                ahead = row > mine[v]
            elif v > other // SUBLANES:
                ahead = row >= mine[v]
            else:
                ahead = (row > mine[v]) | ((row == mine[v]) & (sub > other % SUBLANES))
            rank[v] = rank[v] + jnp.where(ahead, 1.0, 0.0)
    for v in range(n_groups):
        score_t[v * SUBLANES:(v + 1) * SUBLANES, :] = jnp.where(rank[v] < top_k, 1.0, 0.0)
    chosen = score_t[...].T
    sel_ref[0] = jnp.where(valid, chosen, 0.0).astype(sel_ref.dtype)


def _cmp_select(qn, cmp_kv, overlap, seq, batch, gw):
    t = qn.shape[0]
    g = NSA_KV_HEADS
    rep = gw // HEAD_DIM // g
    ncp = cmp_kv.shape[2]
    nsp = overlap.shape[1]
    n_slc = seq // SLC_LEN
    tq = 256
    nq = seq // tq
    body = functools.partial(_cmp_select_body, rep=rep, n_cmp=(seq - CMP_LEN) // CMP_STRIDE + 1,
                             n_slc=n_slc, top_k=min(SLC_TOPK, n_slc))
    return pl.pallas_call(
        body,
        grid=(batch, g, nq),
        in_specs=[pl.BlockSpec((tq, rep * HEAD_DIM), lambda b, gi, i: (b * nq + i, gi)),
                  pl.BlockSpec((1, 1, ncp, HEAD_DIM), lambda b, gi, i: (b, gi, 0, 0)),
                  pl.BlockSpec((1, 1, ncp, HEAD_DIM), lambda b, gi, i: (b, g + gi, 0, 0)),
                  pl.BlockSpec((ncp, nsp), lambda b, gi, i: (0, 0))],
        out_specs=[pl.BlockSpec((tq, rep * HEAD_DIM), lambda b, gi, i: (b * nq + i, gi)),
                   pl.BlockSpec((1, tq, nsp), lambda b, gi, i: (gi, b * nq + i, 0))],
        out_shape=[jax.ShapeDtypeStruct((t, gw), F32), jax.ShapeDtypeStruct((g, t, nsp), BF16)],
        scratch_shapes=[pltpu.VMEM((nsp, tq), F32)],
        compiler_params=_params("parallel", "parallel", "parallel"),
        name="nsa_cmp_select",
    )(qn, cmp_kv, cmp_kv, overlap)


def _attend_body(q_ref, ks_ref, vs_ref, kw_ref, vw_ref, sel_ref, ex_ref, oc_ref, gl_ref, o_ref,
                 s_ref, m_ref, l_ref, acc_ref, *, rep, groups, blk, span):
    i = pl.program_id(1)
    tq = q_ref.shape[0]
    rows = rep * tq
    folds = blk // LANES
    n_blocks = ((i + 1) * tq + blk - 1) // blk

    def stacked_q(g):
        return jnp.concatenate(
            [q_ref[:, (g * rep + r) * HEAD_DIM:(g * rep + r + 1) * HEAD_DIM] for r in range(rep)], axis=0)

    m_ref[...] = jnp.full(m_ref.shape, NEG_BIG, F32)
    l_ref[...] = jnp.zeros(l_ref.shape, F32)
    acc_ref[...] = jnp.zeros(acc_ref.shape, F32)
    tpos = i * tq + lax.broadcasted_iota(jnp.int32, (tq, blk), 0)
    kcol = lax.broadcasted_iota(jnp.int32, (tq, blk), 1)

    def scores(j, carry):
        start = pl.multiple_of(j * blk, blk)
        causal = start + kcol <= tpos
        for g in range(groups):
            cols = slice(g * HEAD_DIM, (g + 1) * HEAD_DIM)
            picked = _dot(sel_ref[g], ex_ref[j])
            bias = jnp.where((picked > 0.5) & causal, 0.0, NEG_BIG)
            s = _dot_nt(stacked_q(g), ks_ref[pl.ds(start, blk), cols]) + jnp.concatenate([bias] * rep, axis=0)
            s_ref[g, j] = s
            m = m_ref[g]
            for c in range(folds):
                m = jnp.maximum(m, s[:, c * LANES:(c + 1) * LANES])
            m_ref[g] = m
        return carry

    lax.fori_loop(0, n_blocks, scores, 0)

    for g in range(groups):
        m_ref[g] = jnp.broadcast_to(jnp.max(m_ref[g], axis=-1, keepdims=True), (rows, LANES))

    def weights(j, carry):
        start = pl.multiple_of(j * blk, blk)
        for g in range(groups):
            cols = slice(g * HEAD_DIM, (g + 1) * HEAD_DIM)
            p = jnp.exp(s_ref[g, j] - jnp.concatenate([m_ref[g]] * folds, axis=1))
            l = l_ref[g]
            for c in range(folds):
                l = l + p[:, c * LANES:(c + 1) * LANES]
            l_ref[g] = l
            acc_ref[g] += _dot(p.astype(BF16), vs_ref[pl.ds(start, blk), cols])
        return carry

    lax.fori_loop(0, n_blocks, weights, 0)

    wstart = pl.multiple_of(jnp.maximum(i * tq + tq - span, 0), tq)
    wpos = i * tq + (lax.broadcasted_iota(jnp.int32, (rep * tq, span), 0) & (tq - 1))
    dist = wpos - (wstart + lax.broadcasted_iota(jnp.int32, (rep * tq, span), 1))
    wbias = jnp.where((dist >= 0) & (dist < WINDOW), 0.0, NEG_BIG)
    gates = _sigmoid(gl_ref[...])

    def gate(head, branch):
        c = 3 * head + branch
        return gates[:, c:c + 1]

    for g in range(groups):
        cols = slice(g * HEAD_DIM, (g + 1) * HEAD_DIM)
        s = _dot_nt(stacked_q(g), kw_ref[pl.ds(wstart, span), cols]) + wbias
        e = jnp.exp(s - jnp.max(s, axis=-1, keepdims=True))
        o_win = _dot(e.astype(BF16), vw_ref[pl.ds(wstart, span), cols]) / jnp.sum(e, axis=-1, keepdims=True)
        o_slc = acc_ref[g] / jnp.sum(l_ref[g], axis=-1, keepdims=True)
        for r in range(rep):
            head = g * rep + r
            hc = slice(head * HEAD_DIM, (head + 1) * HEAD_DIM)
            rows = slice(r * tq, (r + 1) * tq)
            mix = gate(head, 0) * oc_ref[:, hc] + gate(head, 1) * o_slc[rows, :] + gate(head, 2) * o_win[rows, :]
            o_ref[:, hc] = mix.astype(o_ref.dtype)


def _attend(qn, kk, proj, sel, expand, o_cmp, gl, seq, batch, gw, colkv):
    t = qn.shape[0]
    g = NSA_KV_HEADS
    kvw = g * HEAD_DIM
    rep = gw // HEAD_DIM // g
    nsp = sel.shape[2]
    tq = 128
    blk = expand.shape[2]
    nq = seq // tq
    span = WINDOW + tq
    ck = colkv * HEAD_DIM // kvw
    row_spec = pl.BlockSpec((tq, gw), lambda b, i: (b * nq + i, 0))
    return pl.pallas_call(
        functools.partial(_attend_body, rep=rep, groups=g, blk=blk, span=span),
        grid=(batch, nq),
        in_specs=[row_spec,
                  pl.BlockSpec((seq, kvw), lambda b, i: (b, 0)),
                  pl.BlockSpec((seq, kvw), lambda b, i: (b, ck + 3)),
                  pl.BlockSpec((seq, kvw), lambda b, i: (b, 1)),
                  pl.BlockSpec((seq, kvw), lambda b, i: (b, ck + 5)),
                  pl.BlockSpec((g, tq, nsp), lambda b, i: (0, b * nq + i, 0)),
                  pl.BlockSpec((seq // blk, nsp, blk), lambda b, i: (0, 0, 0)),
                  row_spec,
                  pl.BlockSpec((tq, LANES), lambda b, i: (b * nq + i, 0))],
        out_specs=row_spec,
        out_shape=jax.ShapeDtypeStruct((t, gw), BF16),
        scratch_shapes=[pltpu.VMEM((g, seq // blk, rep * tq, blk), F32),
                        pltpu.VMEM((g, rep * tq, LANES), F32), pltpu.VMEM((g, rep * tq, LANES), F32),
                        pltpu.VMEM((g, rep * tq, HEAD_DIM), F32)],
        compiler_params=_params("parallel", "parallel"),
        name="nsa_attend",
    )(qn, kk, proj, kk, proj, sel, expand, o_cmp, gl)


def _nsa(proj, gl, qw3, kw3, pos4, w1, w2, layer, seq, batch, gw, colq, colkv):
    g = NSA_KV_HEADS
    kvw = g * HEAD_DIM
    tables = _rope_tables(jnp.arange(seq))
    qn, kk = _nsa_prep(proj, qw3, kw3, tables, layer, seq, batch, gw, colq, colkv)

    ns = seq // CMP_STRIDE
    n_cmp = (seq - CMP_LEN) // CMP_STRIDE + 1
    cmp_end = jnp.arange(ns) * CMP_STRIDE + CMP_LEN - 1
    cmp_kv = _compress(proj, pos4, w1, w2, kw3, _rope_tables(cmp_end), layer, seq, batch, colkv)

    n_slc = seq // SLC_LEN
    nsp = _round_up(n_slc, LANES)
    ci = np.arange(ns)[:, None] * CMP_STRIDE
    sj = np.arange(nsp)[None, :] * SLC_LEN
    ov = np.maximum(0, np.minimum(ci + CMP_LEN, sj + SLC_LEN) - np.maximum(ci, sj)) / CMP_STRIDE
    ov = np.where((np.arange(ns)[:, None] < n_cmp) & (np.arange(nsp)[None, :] < n_slc), ov, 0.0)
    o_cmp, sel = _cmp_select(qn, cmp_kv, jnp.asarray(ov, BF16), seq, batch, gw)

    blk = 512
    key_blk = (np.arange(seq) // SLC_LEN).reshape(seq // blk, 1, blk)
    expand = jnp.asarray(np.arange(nsp)[None, :, None] == key_blk, BF16)
    return _attend(qn, kk, proj, sel, expand, o_cmp, gl, seq, batch, gw, colkv)


def kernel(x, attn_norm_w, w_in, lru_conv_w, lru_conv_b, lru_w_a, lru_b_a, lru_w_x, lru_b_x, lru_lambda,
           nsa_q_norm_w, nsa_k_norm_w, nsa_cmp_pos, nsa_cmp_w1, nsa_cmp_w2, w_out, mlp_norm_w,
           w_gate, w_up, mlp_conv_w, mlp_conv_b, w_down):
    batch, seq, d = x.shape
    depth = w_in.shape[0]
    gw = d // N_MIXERS
    gh = gw // HEAD_DIM
    kvw = NSA_KV_HEADS * HEAD_DIM
    n_main = 10 * gw + 6 * kvw
    n_gate = w_in.shape[2] - n_main
    t = batch * seq
    assert seq % 512 == 0 and seq >= WINDOW + 128 and gw % (NSA_KV_HEADS * HEAD_DIM) == 0
    assert n_main % 512 == 0 and n_gate <= LANES and d % 1024 == 0 and w_gate.shape[2] % LANES == 0

    w_in_b = (w_in * attn_norm_w[:, :, None]).astype(BF16)
    w_out_b = w_out.astype(BF16)
    w_gate_b = (w_gate * mlp_norm_w[:, :, None]).astype(BF16)
    w_up_b = (w_up * mlp_norm_w[:, :, None]).astype(BF16)
    w_down_b = w_down.astype(BF16)
    mlp_cb3 = mlp_conv_b[:, None, :]
    lru_cb3, lru_ba3, lru_bx3, lru_lam3 = (v[:, None, :] for v in (lru_conv_b, lru_b_a, lru_b_x, lru_lambda))
    lru_wa_b = lru_w_a.astype(BF16)
    lru_wx_b = lru_w_x.astype(BF16)
    qw3 = nsa_q_norm_w[:, None, :]
    half = CMP_LEN // CMP_STRIDE
    pos4 = nsa_cmp_pos.reshape(depth, 2, half, CMP_STRIDE, HEAD_DIM)
    cmp_w1 = nsa_cmp_w1.reshape(depth, 2, half, CMP_STRIDE, HEAD_DIM, -1).astype(BF16)
    cmp_w2 = nsa_cmp_w2.astype(BF16)

    xf = x.reshape(t, d)
    xb, parts = _row_stats(xf)
    for layer in range(depth):
        proj, r = _inproj(xb, w_in_b, parts, layer, n_main, 1024, 512)
        gl = _gate_logits(xb, w_in_b, r, layer, n_main // LANES, 1024)
        y_ret = _retention(proj, seq, batch, gw)
        y_sb = _stickbreak(proj, seq, batch, gw, 4 * gh)
        y_lru = _rglru(proj, lru_conv_w, lru_cb3, lru_wa_b, lru_ba3, lru_wx_b, lru_bx3, lru_lam3,
                       layer, seq, batch, gw, 7 * gh)
        y_nsa = _nsa(proj, gl, qw3, nsa_k_norm_w, pos4, cmp_w1, cmp_w2, layer, seq, batch, gw,
                     9 * gh, 10 * gh)
        xf, xb, r = _outproj((y_ret, y_sb, y_lru, y_nsa), w_out_b, xf, layer, 1024, 512)
        hid = _gateup(xb, r, w_gate_b, w_up_b, mlp_conv_w, mlp_cb3, layer, seq, 512, 1024)
        xf, xb, parts = _down(hid, w_down_b, xf, layer, 256, 1024)
    return xf.reshape(batch, seq, d)
```

```python
import functools
import math

import numpy as np
import jax
import jax.numpy as jnp
from jax import lax
from jax.experimental import pallas as pl
from jax.experimental.pallas import tpu as pltpu

F32 = jnp.float32
BF16 = jnp.bfloat16

HEAD_DIM = 128
N_MIXERS = 4
RET_CHUNK = 128
RET_ROT_BASE = 10000.0
LRU_CONV = 4
LRU_C = 8.0
NSA_KV_HEADS = 2
CMP_LEN = 32
CMP_STRIDE = 16
SLC_LEN = 64
SLC_TOPK = 16
WINDOW = 512
ROPE_THETA = 500000.0
ROPE_DIMS = HEAD_DIM // 4
MLP_CONV = 3
NORM_EPS = 1e-6

LANES = 128
SUBLANES = 8
VMEM_LIMIT_BYTES = 56 * 1024 * 1024
NEG_BIG = -1e30
FORCED_SCORE = 1e30


def _params(*sem, flags=None):
    return pltpu.CompilerParams(dimension_semantics=sem, vmem_limit_bytes=VMEM_LIMIT_BYTES, flags=flags)


def _round_up(x, m):
    return (x + m - 1) // m * m


def _dot(a, b):
    return jnp.dot(a, b, preferred_element_type=F32)


def _dot_nt(a, b):
    return lax.dot_general(a, b, (((1,), (1,)), ((), ())), preferred_element_type=F32)


def _dot_split(x, m):
    hi = x.astype(BF16)
    lo = (x - hi.astype(F32)).astype(BF16)
    return _dot(hi, m) + _dot(lo, m)


def _sigmoid(x):
    return 1.0 / (1.0 + jnp.exp(-x))


def _softplus(x):
    return jnp.maximum(x, 0.0) + jnp.log1p(jnp.exp(-jnp.abs(x)))


def _gelu_tanh(x):
    return x * (0.5 * (1.0 + jnp.tanh(math.sqrt(2.0 / math.pi) * (x + 0.044715 * (x * x * x)))))


def _rms_rows(x):
    return x * lax.rsqrt(jnp.mean(x * x, axis=-1, keepdims=True) + NORM_EPS)


def _fold_lanes(v):
    out = v[:, 0:LANES]
    for c in range(1, v.shape[1] // LANES):
        out = out + v[:, c * LANES:(c + 1) * LANES]
    return out


def _inv_rms(folded, d):
    total = jnp.sum(folded, axis=-1, keepdims=True)
    return jnp.broadcast_to(lax.rsqrt(total * (1.0 / d) + NORM_EPS), folded.shape)


def _widen(r, n):
    return jnp.concatenate([r] * (n // LANES), axis=1)


def _row_stats_body(x_ref, gain_ref, xb_ref, p_ref):
    x = x_ref[...]
    xb_ref[...] = (x * gain_ref[0]).astype(BF16)
    p_ref[0] = _fold_lanes(x * x)


def _row_stats(x, gain3, layer):
    t, d = x.shape
    tr = 256
    return pl.pallas_call(
        _row_stats_body,
        grid=(t // tr,),
        in_specs=[pl.BlockSpec((tr, d), lambda i: (i, 0)), pl.BlockSpec((1, 1, d), lambda i: (layer, 0, 0))],
        out_specs=[pl.BlockSpec((tr, d), lambda i: (i, 0)), pl.BlockSpec((1, tr, LANES), lambda i: (0, i, 0))],
        out_shape=[jax.ShapeDtypeStruct((t, d), BF16), jax.ShapeDtypeStruct((1, t, LANES), F32)],
        compiler_params=_params("parallel"),
        name="row_stats",
    )(x, gain3)


def _inproj_body(a_ref, w_ref, p_ref, o_ref, r_ref, scaled, *, d):
    @pl.when(pl.program_id(1) == 0)
    def _():
        folded = p_ref[0]
        for n in range(1, p_ref.shape[0]):
            folded = folded + p_ref[n]
        r = _inv_rms(folded, d)
        r_ref[...] = r
        scaled[...] = (a_ref[...].astype(F32) * _widen(r, d)).astype(BF16)

    o_ref[...] = _dot(scaled[...], w_ref[0]).astype(o_ref.dtype)


def _inproj(a, w, parts, layer, n, tm, tn):
    t, k = a.shape
    return pl.pallas_call(
        functools.partial(_inproj_body, d=k),
        grid=(t // tm, n // tn),
        in_specs=[pl.BlockSpec((tm, k), lambda i, j: (i, 0)),
                  pl.BlockSpec((1, k, tn), lambda i, j: (layer, 0, j)),
                  pl.BlockSpec((parts.shape[0], tm, LANES), lambda i, j: (0, i, 0))],
        out_specs=[pl.BlockSpec((tm, tn), lambda i, j: (i, j)), pl.BlockSpec((tm, LANES), lambda i, j: (i, 0))],
        out_shape=[jax.ShapeDtypeStruct((t, n), BF16), jax.ShapeDtypeStruct((t, LANES), F32)],
        scratch_shapes=[pltpu.VMEM((tm, k), BF16)],
        compiler_params=_params("parallel", "arbitrary"),
        name="inproj",
    )(a, w, parts)


def _gate_logits_body(a_ref, w_ref, r_ref, o_ref):
    o_ref[...] = _dot(a_ref[...], w_ref[0]) * r_ref[...]


def _gate_logits(a, w, r, layer, col, tm):
    t, k = a.shape
    return pl.pallas_call(
        _gate_logits_body,
        grid=(t // tm,),
        in_specs=[pl.BlockSpec((tm, k), lambda i: (i, 0)),
                  pl.BlockSpec((1, k, LANES), lambda i: (layer, 0, col)),
                  pl.BlockSpec((tm, LANES), lambda i: (i, 0))],
        out_specs=pl.BlockSpec((tm, LANES), lambda i: (i, 0)),
        out_shape=jax.ShapeDtypeStruct((t, LANES), F32),
        compiler_params=_params("parallel"),
        name="inproj_gates",
    )(a, w, r)


def _outproj_body(y0_ref, y1_ref, y2_ref, y3_ref, w_ref, x_ref, gain_ref, o_ref, ob_ref, r_ref, ssq, *, d):
    j = pl.program_id(1)
    gw = y0_ref.shape[1]
    acc = x_ref[...]
    for n, y_ref in enumerate((y0_ref, y1_ref, y2_ref, y3_ref)):
        acc = acc + _dot(y_ref[...], w_ref[0, n * gw:(n + 1) * gw, :])
    o_ref[...] = acc
    ob_ref[...] = (acc * gain_ref[0]).astype(BF16)
    folded = _fold_lanes(acc * acc)

    @pl.when(j == 0)
    def _():
        ssq[...] = folded

    @pl.when(j > 0)
    def _():
        ssq[...] += folded

    @pl.when(j == pl.num_programs(1) - 1)
    def _():
        r_ref[...] = _inv_rms(ssq[...], d)


def _outproj(ys, w, x, gain3, layer, tm, tn):
    t, d = x.shape
    gw = ys[0].shape[1]
    y_spec = pl.BlockSpec((tm, gw), lambda i, j: (i, 0))
    tile = pl.BlockSpec((tm, tn), lambda i, j: (i, j))
    return pl.pallas_call(
        functools.partial(_outproj_body, d=d),
        grid=(t // tm, d // tn),
        in_specs=[y_spec, y_spec, y_spec, y_spec,
                  pl.BlockSpec((1, N_MIXERS * gw, tn), lambda i, j: (layer, 0, j)), tile,
                  pl.BlockSpec((1, 1, tn), lambda i, j: (layer, 0, j))],
        out_specs=[tile, tile, pl.BlockSpec((tm, LANES), lambda i, j: (i, 0))],
        out_shape=[jax.ShapeDtypeStruct((t, d), F32), jax.ShapeDtypeStruct((t, d), BF16),
                   jax.ShapeDtypeStruct((t, LANES), F32)],
        scratch_shapes=[pltpu.VMEM((tm, LANES), F32)],
        compiler_params=_params("parallel", "arbitrary"),
        name="outproj",
    )(*ys, w, x, gain3)


def _gateup_body(a_ref, r_ref, wg_ref, wu_ref, cw_ref, cb_ref, o_ref, gbuf, *, tiles_per_seq):
    i = pl.program_id(1)
    tm = a_ref.shape[0]
    tn = o_ref.shape[1]

    @pl.when(i % tiles_per_seq == 0)
    def _():
        gbuf[0:SUBLANES, :] = jnp.zeros((SUBLANES, tn), F32)

    r = _widen(r_ref[...], tn)
    g = _dot(a_ref[...], wg_ref[0]) * r
    u = _dot(a_ref[...], wu_ref[0]) * r
    gbuf[SUBLANES:SUBLANES + tm, :] = g
    cw = cw_ref[0]
    gt = (cb_ref[0] + cw[2:3, :] * g
          + cw[1:2, :] * gbuf[SUBLANES - 1:SUBLANES - 1 + tm, :]
          + cw[0:1, :] * gbuf[SUBLANES - 2:SUBLANES - 2 + tm, :])
    o_ref[...] = (gt * _sigmoid(gt) * u).astype(o_ref.dtype)
    gbuf[0:SUBLANES, :] = g[tm - SUBLANES:tm, :]


def _gateup(a, r, wg, wu, cw, cb3, layer, seq, tm, tn):
    t, k = a.shape
    f = wg.shape[2]
    w_spec = pl.BlockSpec((1, k, tn), lambda j, i: (layer, 0, j))
    return pl.pallas_call(
        functools.partial(_gateup_body, tiles_per_seq=seq // tm),
        grid=(pl.cdiv(f, tn), t // tm),
        in_specs=[pl.BlockSpec((tm, k), lambda j, i: (i, 0)),
                  pl.BlockSpec((tm, LANES), lambda j, i: (i, 0)), w_spec, w_spec,
                  pl.BlockSpec((1, MLP_CONV, tn), lambda j, i: (layer, 0, j)),
                  pl.BlockSpec((1, 1, tn), lambda j, i: (layer, 0, j))],
        out_specs=pl.BlockSpec((tm, tn), lambda j, i: (i, j)),
        out_shape=jax.ShapeDtypeStruct((t, f), BF16),
        scratch_shapes=[pltpu.VMEM((SUBLANES + tm, tn), F32)],
        compiler_params=_params("parallel", "arbitrary"),
        name="mlp_gateup",
    )(a, r, wg, wu, cw, cb3)


def _down_body(h_ref, w_ref, x_ref, gain_ref, o_ref, ob_ref, p_ref):
    acc = x_ref[...] + _dot(h_ref[...], w_ref[0])
    o_ref[...] = acc
    ob_ref[...] = (acc * gain_ref[0]).astype(BF16)
    p_ref[0] = _fold_lanes(acc * acc)


def _down(h, w, x, gain3, layer, gain_layer, tm, tn):
    t, f = h.shape
    d = x.shape[1]
    tile = pl.BlockSpec((tm, tn), lambda j, i: (i, j))
    return pl.pallas_call(
        _down_body,
        grid=(d // tn, t // tm),
        in_specs=[pl.BlockSpec((tm, f), lambda j, i: (i, 0)),
                  pl.BlockSpec((1, f, tn), lambda j, i: (layer, 0, j), pipeline_mode=pl.Buffered(1)),
                  tile,
                  pl.BlockSpec((1, 1, tn), lambda j, i: (gain_layer, 0, j))],
        out_specs=[tile, tile, pl.BlockSpec((1, tm, LANES), lambda j, i: (j, i, 0))],
        out_shape=[jax.ShapeDtypeStruct((t, d), F32), jax.ShapeDtypeStruct((t, d), BF16),
                   jax.ShapeDtypeStruct((d // tn, t, LANES), F32)],
        compiler_params=_params("parallel", "parallel"),
        name="mlp_down",
    )(h, w, x, gain3)


def _retention_body(q_ref, k_ref, v_ref, g_ref, cos_ref, sin_ref, inner_ref, qd_ref, kd_ref, cd_ref,
                    o_ref, state, *, heads):
    @pl.when(pl.program_id(1) == 0)
    def _():
        state[...] = jnp.zeros(state.shape, F32)

    cos = cos_ref[...]
    sin = sin_ref[...]

    def rot(x):
        return x * cos + pltpu.roll(x, HEAD_DIM // 2, axis=1) * sin

    for h in range(heads):
        cols = slice(h * HEAD_DIM, (h + 1) * HEAD_DIM)
        q = rot(q_ref[:, cols].astype(F32))
        k = rot(k_ref[:, cols].astype(F32)) * (HEAD_DIM ** -0.5)
        vb = v_ref[:, cols]
        qb = q.astype(BF16)
        inner = _dot_nt(qb, k.astype(BF16)) * inner_ref[h]
        st = state[h]
        o = _dot(inner.astype(BF16), vb) + _dot(qb, st.astype(BF16)) * qd_ref[h]
        kd = (k * kd_ref[h]).T.astype(BF16)
        state[h] = st * cd_ref[h] + _dot(kd, vb)
        g = g_ref[:, cols].astype(F32)
        o_ref[:, cols] = (_rms_rows(o) * (g * _sigmoid(g))).astype(o_ref.dtype)


def _retention(proj, seq, batch, gw):
    t = proj.shape[0]
    heads = gw // HEAD_DIM
    c = RET_CHUNK
    nc = seq // c
    pos = jnp.arange(seq)
    inv = 1.0 / (RET_ROT_BASE ** jnp.linspace(0.0, 1.0, HEAD_DIM // 2, dtype=F32))
    ang = pos.astype(F32)[:, None] * inv[None, :]
    cos, sin = jnp.cos(ang), jnp.sin(ang)
    cos_t = jnp.concatenate([cos, cos], axis=1)
    sin_t = jnp.concatenate([-sin, sin], axis=1)
    log_g = jnp.log(1.0 - 2.0 ** (-5.0 - jnp.arange(heads, dtype=F32)))
    idx = jnp.arange(c, dtype=F32)
    rel = idx[:, None] - idx[None, :]
    inner_decay = jnp.where(rel >= 0, jnp.exp(log_g[:, None, None] * jnp.maximum(rel, 0.0)), 0.0)
    q_decay = jnp.broadcast_to(jnp.exp(log_g[:, None] * (idx + 1.0))[..., None], (heads, c, HEAD_DIM))
    k_decay = jnp.broadcast_to(jnp.exp(log_g[:, None] * (c - 1.0 - idx))[..., None], (heads, c, HEAD_DIM))
    chunk_decay = jnp.broadcast_to(jnp.exp(log_g * c)[:, None, None], (heads, HEAD_DIM, HEAD_DIM))

    def col(n):
        return pl.BlockSpec((c, gw), lambda b, i: (b * nc + i, n))

    rope_spec = pl.BlockSpec((c, HEAD_DIM), lambda b, i: (i, 0))
    const_spec = pl.BlockSpec((heads, c, HEAD_DIM), lambda b, i: (0, 0, 0))
    return pl.pallas_call(
        functools.partial(_retention_body, heads=heads),
        grid=(batch, nc),
        in_specs=[col(0), col(1), col(2), col(3), rope_spec, rope_spec,
                  const_spec, const_spec, const_spec, const_spec],
        out_specs=pl.BlockSpec((c, gw), lambda b, i: (b * nc + i, 0)),
        out_shape=jax.ShapeDtypeStruct((t, gw), BF16),
        scratch_shapes=[pltpu.VMEM((heads, HEAD_DIM, HEAD_DIM), F32)],
        compiler_params=_params("parallel", "arbitrary"),
        name="retention",
    )(proj, proj, proj, proj, cos_t, sin_t, inner_decay, q_decay, k_decay, chunk_decay)


def _stickbreak_body(q_ref, k_ref, v_ref, tri_ref, o_ref, acc_ref, later_ref, *, bk, heads):
    i = pl.program_id(2)
    tq = q_ref.shape[0]
    tri = tri_ref[...]
    log2e = 1.0 / math.log(2.0)
    acc_ref[...] = jnp.zeros(acc_ref.shape, F32)
    later_ref[...] = jnp.zeros(later_ref.shape, F32)
    sign_bit = jnp.uint32(0x80000000)

    def tile(j, row0):
        masked = row0 is not None
        r0 = row0 if masked else 0
        nr = tq - r0
        start = pl.multiple_of(j * bk, bk)
        if masked:
            qpos = i * tq + r0 + lax.broadcasted_iota(jnp.int32, (nr, bk), 0)
            past = start + lax.broadcasted_iota(jnp.int32, (nr, bk), 1) < qpos
        for h in range(heads):
            cols = slice(h * HEAD_DIM, (h + 1) * HEAD_DIM)
            y = _dot_nt(q_ref[r0:, cols], k_ref[pl.ds(start, bk), cols]) * (HEAD_DIM ** -0.5 * log2e)
            minus_abs = pltpu.bitcast(pltpu.bitcast(y, jnp.uint32) | sign_bit, F32)
            stay = jnp.maximum(y, 0.0) + jnp.log(1.0 + jnp.exp2(minus_abs)) * log2e
            if masked:
                stay = jnp.where(past, stay, 0.0)
            hi = stay.astype(BF16)
            both = _dot(hi, tri)
            later = later_ref[h, r0:, :]
            w = jnp.exp2(y - (both + later))
            if masked:
                w = jnp.where(past, w, 0.0)
            acc_ref[h, r0:, :] += _dot(w.astype(BF16), v_ref[pl.ds(start, bk), cols])
            later_ref[h, r0:, :] = later + jnp.sum(stay, axis=1, keepdims=True)

    n_diag = tq // bk
    for d in reversed(range(n_diag)):
        tile(i * n_diag + d, d * bk)

    def body(jj, c):
        tile(i * n_diag - 1 - jj, None)
        return c

    lax.fori_loop(0, i * n_diag, body, 0)
    for h in range(heads):
        o_ref[:, h * HEAD_DIM:(h + 1) * HEAD_DIM] = acc_ref[h].astype(o_ref.dtype)


def _stickbreak(proj, seq, batch, gw, col0):
    t = proj.shape[0]
    tq, bk = 512, 256
    nq = seq // tq
    hb = gw // HEAD_DIM
    wb = hb * HEAD_DIM
    c0 = col0 * HEAD_DIM // wb
    per = gw // wb
    ii = np.arange(bk)
    tri = jnp.asarray(ii[:, None] >= ii[None, :], BF16)
    kv_spec = lambda n: pl.BlockSpec((seq, wb), lambda b, h, i: (b, c0 + n * per + h))
    return pl.pallas_call(
        functools.partial(_stickbreak_body, bk=bk, heads=hb),
        grid=(batch, per, nq),
        in_specs=[pl.BlockSpec((tq, wb), lambda b, h, i: (b * nq + i, c0 + h)),
                  kv_spec(1), kv_spec(2),
                  pl.BlockSpec((bk, bk), lambda b, h, i: (0, 0))],
        out_specs=pl.BlockSpec((tq, wb), lambda b, h, i: (b * nq + i, h)),
        out_shape=jax.ShapeDtypeStruct((t, gw), BF16),
        scratch_shapes=[pltpu.VMEM((hb, tq, HEAD_DIM), F32), pltpu.VMEM((hb, tq, 1), F32)],
        compiler_params=_params("parallel", "parallel", "arbitrary"),
        name="stickbreak",
    )(proj, proj, proj, tri)


def _rglru_body(gate_ref, rec_ref, cw_ref, cb_ref, wa_ref, ba_ref, wx_ref, bx_ref, lam_ref, o_ref,
                xbuf, abuf, ubuf, hprev, *, blocks):
    i = pl.program_id(1)
    ts = rec_ref.shape[0]
    gw = rec_ref.shape[1]

    @pl.when(i == 0)
    def _():
        xbuf[0:SUBLANES, :] = jnp.zeros((SUBLANES, gw), F32)
        hprev[...] = jnp.zeros(hprev.shape, F32)

    x = rec_ref[...].astype(F32)
    xbuf[SUBLANES:SUBLANES + ts, :] = x
    cw = cw_ref[0]
    xr = cb_ref[0] + cw[3:4, :] * x
    for k in range(LRU_CONV - 1):
        off = SUBLANES - (LRU_CONV - 1) + k
        xr = xr + cw[k:k + 1, :] * xbuf[off:off + ts, :]
    xbuf[0:SUBLANES, :] = x[ts - SUBLANES:ts, :]

    log_sig_lam = -_softplus(-lam_ref[0])
    xrb = xr.astype(BF16)
    bw = gw // blocks
    first = (lax.broadcasted_iota(jnp.int32, (ts, bw), 0) == 0) & (i == 0)
    for n in range(blocks):
        cols = slice(n * bw, (n + 1) * bw)
        r = _sigmoid(_dot(xrb[:, cols], wa_ref[0, n]) + ba_ref[0, :, cols])
        gi = _sigmoid(_dot(xrb[:, cols], wx_ref[0, n]) + bx_ref[0, :, cols])
        log_a = LRU_C * r * log_sig_lam[:, cols]
        mult = jnp.where(first, 1.0, jnp.sqrt(jnp.maximum(1.0 - jnp.exp(2.0 * log_a), 0.0)))
        abuf[:, cols] = jnp.exp(log_a)
        ubuf[:, cols] = mult * (gi * xr[:, cols])

    rows = lax.broadcasted_iota(jnp.int32, (SUBLANES, gw), 0)

    def group(gidx, h):
        start = pl.multiple_of(gidx * SUBLANES, SUBLANES)
        a = abuf[pl.ds(start, SUBLANES), :]
        u = ubuf[pl.ds(start, SUBLANES), :]
        for s in (1, 2, 4):
            keep = rows >= s
            a_sh = jnp.where(keep, pltpu.roll(a, s, axis=0), 1.0)
            u_sh = jnp.where(keep, pltpu.roll(u, s, axis=0), 0.0)
            u = u + a * u_sh
            a = a * a_sh
        hs = u + a * h
        ubuf[pl.ds(start, SUBLANES), :] = hs
        return jnp.broadcast_to(hs[SUBLANES - 1:SUBLANES, :], (SUBLANES, gw))

    hprev[...] = lax.fori_loop(0, ts // SUBLANES, group, hprev[...])
    o_ref[...] = (ubuf[...] * _gelu_tanh(gate_ref[...].astype(F32))).astype(o_ref.dtype)


def _rglru(proj, cw, cb3, wa, ba3, wx, bx3, lam3, layer, seq, batch, gw, col0):
    t = proj.shape[0]
    blocks = gw // HEAD_DIM
    ts = 256
    ns = seq // ts
    cpb = col0 // blocks
    vec_spec = pl.BlockSpec((1, 1, gw), lambda b, i: (layer, 0, 0))
    w_spec = pl.BlockSpec((1, blocks, HEAD_DIM, HEAD_DIM), lambda b, i: (layer, 0, 0, 0))
    return pl.pallas_call(
        functools.partial(_rglru_body, blocks=blocks),
        grid=(batch, ns),
        in_specs=[pl.BlockSpec((ts, gw), lambda b, i: (b * ns + i, cpb)),
                  pl.BlockSpec((ts, gw), lambda b, i: (b * ns + i, cpb + 1)),
                  pl.BlockSpec((1, LRU_CONV, gw), lambda b, i: (layer, 0, 0)),
                  vec_spec, w_spec, vec_spec, w_spec, vec_spec, vec_spec],
        out_specs=pl.BlockSpec((ts, gw), lambda b, i: (b * ns + i, 0)),
        out_shape=jax.ShapeDtypeStruct((t, gw), BF16),
        scratch_shapes=[pltpu.VMEM((SUBLANES + ts, gw), F32), pltpu.VMEM((ts, gw), F32),
                        pltpu.VMEM((ts, gw), F32), pltpu.VMEM((SUBLANES, gw), F32)],
        compiler_params=_params("parallel", "arbitrary"),
        name="rglru",
    )(proj, proj, cw, cb3, wa, ba3, wx, bx3, lam3)


def _rope_tables(pos):
    half = ROPE_DIMS // 2
    inv = ROPE_THETA ** (-jnp.arange(half, dtype=F32) / half)
    ang = pos.astype(F32)[:, None] * inv[None, :]
    cos, sin = jnp.cos(ang), jnp.sin(ang)
    n = pos.shape[0]
    zeros = jnp.zeros((n, half), F32)
    rest = HEAD_DIM - 2 * half
    c = jnp.concatenate([cos, cos, jnp.ones((n, rest), F32)], axis=1)
    s_lo = jnp.concatenate([-sin, zeros, jnp.zeros((n, rest), F32)], axis=1)
    s_hi = jnp.concatenate([zeros, sin, jnp.zeros((n, rest), F32)], axis=1)
    return c, s_lo, s_hi


def _partial_rope(x, c, s_lo, s_hi):
    half = ROPE_DIMS // 2
    return x * c + pltpu.roll(x, HEAD_DIM - half, axis=1) * s_lo + pltpu.roll(x, half, axis=1) * s_hi


def _nsa_prep_body(q_ref, ks_ref, kw_ref, qw_ref, kwt_ref, c_ref, lo_ref, hi_ref, qo_ref, ko_ref,
                   *, heads, kv_heads):
    c, s_lo, s_hi = c_ref[...], lo_ref[...], hi_ref[...]

    def prep(x, w):
        return _partial_rope(_rms_rows(x) * w, c, s_lo, s_hi)

    qw = qw_ref[0]
    for h in range(heads):
        cols = slice(h * HEAD_DIM, (h + 1) * HEAD_DIM)
        qo_ref[:, cols] = (prep(q_ref[:, cols].astype(F32), qw) * (HEAD_DIM ** -0.5)).astype(qo_ref.dtype)
    for n, src in enumerate((ks_ref, kw_ref)):
        w = kwt_ref[0, n + 1:n + 2, :]
        for g in range(kv_heads):
            cols = slice(g * HEAD_DIM, (g + 1) * HEAD_DIM)
            dst = slice((n * kv_heads + g) * HEAD_DIM, (n * kv_heads + g + 1) * HEAD_DIM)
            ko_ref[:, dst] = prep(src[:, cols].astype(F32), w).astype(ko_ref.dtype)


def _nsa_prep(proj, qw3, kw3, tables, layer, seq, batch, gw, colq, colkv):
    t = proj.shape[0]
    heads = gw // HEAD_DIM
    kvw = NSA_KV_HEADS * HEAD_DIM
    ts = 256
    ns = seq // ts
    rope_spec = pl.BlockSpec((ts, HEAD_DIM), lambda b, i: (i, 0))
    cq = colq * HEAD_DIM // gw
    ck = colkv * HEAD_DIM // kvw
    return pl.pallas_call(
        functools.partial(_nsa_prep_body, heads=heads, kv_heads=NSA_KV_HEADS),
        grid=(batch, ns),
        in_specs=[pl.BlockSpec((ts, gw), lambda b, i: (b * ns + i, cq)),
                  pl.BlockSpec((ts, kvw), lambda b, i: (b * ns + i, ck + 2)),
                  pl.BlockSpec((ts, kvw), lambda b, i: (b * ns + i, ck + 4)),
                  pl.BlockSpec((1, 1, HEAD_DIM), lambda b, i: (layer, 0, 0)),
                  pl.BlockSpec((1, 3, HEAD_DIM), lambda b, i: (layer, 0, 0)),
                  rope_spec, rope_spec, rope_spec],
        out_specs=[pl.BlockSpec((ts, gw), lambda b, i: (b * ns + i, 0)),
                   pl.BlockSpec((ts, 2 * kvw), lambda b, i: (b * ns + i, 0))],
        out_shape=[jax.ShapeDtypeStruct((t, gw), BF16), jax.ShapeDtypeStruct((t, 2 * kvw), BF16)],
        compiler_params=_params("parallel", "parallel"),
        name="nsa_prep",
    )(proj, proj, proj, qw3, kw3, *tables)


def _compress_body(x_ref, pos_ref, w1_ref, w2_ref, kw_ref, c_ref, lo_ref, hi_ref, o_ref, xs):
    xs[...] = x_ref[...].astype(F32)
    ns = o_ref.shape[2]
    hidden = w2_ref.shape[2]
    first = jnp.zeros((ns, hidden), F32)
    second = jnp.zeros((ns, hidden), F32)
    for l in range(CMP_STRIDE):
        rows = xs[pl.ds(l, ns, stride=CMP_STRIDE), :]
        first = first + _dot((rows + pos_ref[0, 0, 0, l:l + 1, :]).astype(BF16), w1_ref[0, 0, 0, l])
        second = second + _dot((rows + pos_ref[0, 0, 1, l:l + 1, :]).astype(BF16), w1_ref[0, 0, 1, l])
    hid = _gelu_tanh(first + pltpu.roll(second, ns - 1, axis=0))
    out = _dot(hid.astype(BF16), w2_ref[0, 0])

    @pl.when(pl.program_id(1) < NSA_KV_HEADS)
    def _():
        o_ref[0, 0] = _partial_rope(_rms_rows(out) * kw_ref[0, 0:1, :], c_ref[...], lo_ref[...], hi_ref[...])

    @pl.when(pl.program_id(1) >= NSA_KV_HEADS)
    def _():
        o_ref[0, 0] = out


def _compress(proj, pos5, w1, w2, kw3, tables, layer, seq, batch, colkv):
    g = NSA_KV_HEADS
    n4 = 2 * g
    ns = seq // CMP_STRIDE
    half = CMP_LEN // CMP_STRIDE
    hidden = w1.shape[-1]
    full = pl.BlockSpec((ns, HEAD_DIM), lambda b, c: (0, 0))
    return pl.pallas_call(
        _compress_body,
        grid=(batch, n4),
        in_specs=[pl.BlockSpec((seq, HEAD_DIM), lambda b, c: (b, colkv + c)),
                  pl.BlockSpec((1, 1, half, CMP_STRIDE, HEAD_DIM), lambda b, c: (layer, c // g, 0, 0, 0)),
                  pl.BlockSpec((1, 1, half, CMP_STRIDE, HEAD_DIM, hidden),
                               lambda b, c: (layer, c // g, 0, 0, 0, 0)),
                  pl.BlockSpec((1, 1, hidden, HEAD_DIM), lambda b, c: (layer, c // g, 0, 0)),
                  pl.BlockSpec((1, 3, HEAD_DIM), lambda b, c: (layer, 0, 0)),
                  full, full, full],
        out_specs=pl.BlockSpec((1, 1, ns, HEAD_DIM), lambda b, c: (b, c, 0, 0)),
        out_shape=jax.ShapeDtypeStruct((batch, n4, ns, HEAD_DIM), F32),
        scratch_shapes=[pltpu.VMEM((seq, HEAD_DIM), F32)],
        compiler_params=_params("parallel", "parallel"),
        name="nsa_compress",
    )(proj, pos5, w1, w2, kw3, *tables)


def _cmp_select_body(q_ref, kc_ref, vc_ref, ov_ref, o_ref, sel_ref, score_t, *, rep, n_cmp, n_slc, top_k):
    i = pl.program_id(2)
    tq = q_ref.shape[0]
    ncp = kc_ref.shape[2]
    nsp = ov_ref.shape[1]
    kc = kc_ref[0, 0].astype(BF16)
    vc = vc_ref[0, 0].astype(BF16)
    tpos = i * tq + lax.broadcasted_iota(jnp.int32, (tq, ncp), 0)
    blk_n = lax.broadcasted_iota(jnp.int32, (tq, ncp), 1)
    visible = (blk_n * CMP_STRIDE + (CMP_LEN - 1) <= tpos) & (blk_n < n_cmp)
    p_sum = jnp.zeros((tq, ncp), F32)
    for r in range(rep):
        cols = slice(r * HEAD_DIM, (r + 1) * HEAD_DIM)
        s = jnp.where(visible, _dot_nt(q_ref[:, cols], kc), -jnp.inf)
        m = jnp.max(s, axis=-1, keepdims=True)
        m = jnp.where(m > -jnp.inf, m, 0.0)
        e = jnp.where(visible, jnp.exp(s - m), 0.0)
        p = e / jnp.maximum(jnp.sum(e, axis=-1, keepdims=True), 1e-30)
        o_ref[:, cols] = _dot(p.astype(BF16), vc)
        p_sum = p_sum + p
    p_slc = _dot_split(p_sum, ov_ref[...])

    tpos_s = i * tq + lax.broadcasted_iota(jnp.int32, (tq, nsp), 0)
    blk_s = lax.broadcasted_iota(jnp.int32, (tq, nsp), 1)
    cur = tpos_s // SLC_LEN
    forced = (blk_s == 0) | (blk_s == cur) | (blk_s == cur - 1)
    valid = blk_s <= cur
    score = jnp.where(forced, FORCED_SCORE, jnp.where(valid, p_slc, NEG_BIG))
    score_t[...] = score.T
    n_groups = _round_up(n_slc, SUBLANES) // SUBLANES
    mine = [score_t[v * SUBLANES:(v + 1) * SUBLANES, :] for v in range(n_groups)]
    rank = [jnp.zeros((SUBLANES, tq), F32) for _ in range(n_groups)]
    sub = lax.broadcasted_iota(jnp.int32, (SUBLANES, tq), 0)
    for other in range(n_slc):
        row = score_t[other:other + 1, :]
        for v in range(n_groups):
            if v < other // SUBLANES:
                ahead = row > mine[v]
            elif v > other // SUBLANES:
                ahead = row >= mine[v]
            else:
                ahead = (row > mine[v]) | ((row == mine[v]) & (sub > other % SUBLANES))
            rank[v] = rank[v] + jnp.where(ahead, 1.0, 0.0)
    for v in range(n_groups):
        score_t[v * SUBLANES:(v + 1) * SUBLANES, :] = jnp.where(rank[v] < top_k, 1.0, 0.0)
    chosen = score_t[...].T
    sel_ref[0] = jnp.where(valid, chosen, 0.0).astype(sel_ref.dtype)


def _cmp_select(qn, cmp_kv, overlap, seq, batch, gw):
    t = qn.shape[0]
    g = NSA_KV_HEADS
    rep = gw // HEAD_DIM // g
    ncp = cmp_kv.shape[2]
    nsp = overlap.shape[1]
    n_slc = seq // SLC_LEN
    tq = 256
    nq = seq // tq
    body = functools.partial(_cmp_select_body, rep=rep, n_cmp=(seq - CMP_LEN) // CMP_STRIDE + 1,
                             n_slc=n_slc, top_k=min(SLC_TOPK, n_slc))
    return pl.pallas_call(
        body,
        grid=(batch, g, nq),
        in_specs=[pl.BlockSpec((tq, rep * HEAD_DIM), lambda b, gi, i: (b * nq + i, gi)),
                  pl.BlockSpec((1, 1, ncp, HEAD_DIM), lambda b, gi, i: (b, gi, 0, 0)),
                  pl.BlockSpec((1, 1, ncp, HEAD_DIM), lambda b, gi, i: (b, g + gi, 0, 0)),
                  pl.BlockSpec((ncp, nsp), lambda b, gi, i: (0, 0))],
        out_specs=[pl.BlockSpec((tq, rep * HEAD_DIM), lambda b, gi, i: (b * nq + i, gi)),
                   pl.BlockSpec((1, tq, nsp), lambda b, gi, i: (gi, b * nq + i, 0))],
        out_shape=[jax.ShapeDtypeStruct((t, gw), F32), jax.ShapeDtypeStruct((g, t, nsp), BF16)],
        scratch_shapes=[pltpu.VMEM((nsp, tq), F32)],
        compiler_params=_params("parallel", "parallel", "parallel"),
        name="nsa_cmp_select",
    )(qn, cmp_kv, cmp_kv, overlap)


def _attend_body(q_ref, ks_ref, vs_ref, kw_ref, vw_ref, sel_ref, ex_ref, oc_ref, gl_ref, o_ref,
                 s_ref, m_ref, l_ref, acc_ref, *, rep, groups, blk, span):
    i = pl.program_id(1)
    tq = q_ref.shape[0]
    rows = rep * tq
    folds = blk // LANES
    n_blocks = ((i + 1) * tq + blk - 1) // blk

    def stacked_q(g):
        return jnp.concatenate(
            [q_ref[:, (g * rep + r) * HEAD_DIM:(g * rep + r + 1) * HEAD_DIM] for r in range(rep)], axis=0)

    m_ref[...] = jnp.full(m_ref.shape, NEG_BIG, F32)
    l_ref[...] = jnp.zeros(l_ref.shape, F32)
    acc_ref[...] = jnp.zeros(acc_ref.shape, F32)
    tpos = i * tq + lax.broadcasted_iota(jnp.int32, (tq, blk), 0)
    kcol = lax.broadcasted_iota(jnp.int32, (tq, blk), 1)

    def scores(j, carry):
        start = pl.multiple_of(j * blk, blk)
        causal = start + kcol <= tpos
        for g in range(groups):
            cols = slice(g * HEAD_DIM, (g + 1) * HEAD_DIM)
            picked = _dot(sel_ref[g], ex_ref[j])
            bias = jnp.where((picked > 0.5) & causal, 0.0, NEG_BIG)
            s = _dot_nt(stacked_q(g), ks_ref[pl.ds(start, blk), cols]) + jnp.concatenate([bias] * rep, axis=0)
            s_ref[g, j] = s
            m = m_ref[g]
            for c in range(folds):
                m = jnp.maximum(m, s[:, c * LANES:(c + 1) * LANES])
            m_ref[g] = m
        return carry

    lax.fori_loop(0, n_blocks, scores, 0)

    for g in range(groups):
        m_ref[g] = jnp.broadcast_to(jnp.max(m_ref[g], axis=-1, keepdims=True), (rows, LANES))

    def weights(j, carry):
        start = pl.multiple_of(j * blk, blk)
        for g in range(groups):
            cols = slice(g * HEAD_DIM, (g + 1) * HEAD_DIM)
            p = jnp.exp(s_ref[g, j] - jnp.concatenate([m_ref[g]] * folds, axis=1))
            l = l_ref[g]
            for c in range(folds):
                l = l + p[:, c * LANES:(c + 1) * LANES]
            l_ref[g] = l
            acc_ref[g] += _dot(p.astype(BF16), vs_ref[pl.ds(start, blk), cols])
        return carry

    lax.fori_loop(0, n_blocks, weights, 0)

    wstart = pl.multiple_of(jnp.maximum(i * tq + tq - span, 0), tq)
    wpos = i * tq + (lax.broadcasted_iota(jnp.int32, (rep * tq, span), 0) & (tq - 1))
    dist = wpos - (wstart + lax.broadcasted_iota(jnp.int32, (rep * tq, span), 1))
    wbias = jnp.where((dist >= 0) & (dist < WINDOW), 0.0, NEG_BIG)
    gates = _sigmoid(gl_ref[...])

    def gate(head, branch):
        c = 3 * head + branch
        return gates[:, c:c + 1]

    for g in range(groups):
        cols = slice(g * HEAD_DIM, (g + 1) * HEAD_DIM)
        s = _dot_nt(stacked_q(g), kw_ref[pl.ds(wstart, span), cols]) + wbias
        e = jnp.exp(s - jnp.max(s, axis=-1, keepdims=True))
        o_win = _dot(e.astype(BF16), vw_ref[pl.ds(wstart, span), cols]) / jnp.sum(e, axis=-1, keepdims=True)
        o_slc = acc_ref[g] / jnp.sum(l_ref[g], axis=-1, keepdims=True)
        for r in range(rep):
            head = g * rep + r
            hc = slice(head * HEAD_DIM, (head + 1) * HEAD_DIM)
            rows = slice(r * tq, (r + 1) * tq)
            mix = gate(head, 0) * oc_ref[:, hc] + gate(head, 1) * o_slc[rows, :] + gate(head, 2) * o_win[rows, :]
            o_ref[:, hc] = mix.astype(o_ref.dtype)


def _attend(qn, kk, proj, sel, expand, o_cmp, gl, seq, batch, gw, colkv):
    t = qn.shape[0]
    g = NSA_KV_HEADS
    kvw = g * HEAD_DIM
    rep = gw // HEAD_DIM // g
    nsp = sel.shape[2]
    tq = 128
    blk = expand.shape[2]
    nq = seq // tq
    span = WINDOW + tq
    ck = colkv * HEAD_DIM // kvw
    row_spec = pl.BlockSpec((tq, gw), lambda b, i: (b * nq + i, 0))
    return pl.pallas_call(
        functools.partial(_attend_body, rep=rep, groups=g, blk=blk, span=span),
        grid=(batch, nq),
        in_specs=[row_spec,
                  pl.BlockSpec((seq, kvw), lambda b, i: (b, 0)),
                  pl.BlockSpec((seq, kvw), lambda b, i: (b, ck + 3)),
                  pl.BlockSpec((seq, kvw), lambda b, i: (b, 1)),
                  pl.BlockSpec((seq, kvw), lambda b, i: (b, ck + 5)),
                  pl.BlockSpec((g, tq, nsp), lambda b, i: (0, b * nq + i, 0)),
                  pl.BlockSpec((seq // blk, nsp, blk), lambda b, i: (0, 0, 0)),
                  row_spec,
                  pl.BlockSpec((tq, LANES), lambda b, i: (b * nq + i, 0))],
        out_specs=row_spec,
        out_shape=jax.ShapeDtypeStruct((t, gw), BF16),
        scratch_shapes=[pltpu.VMEM((g, seq // blk, rep * tq, blk), F32),
                        pltpu.VMEM((g, rep * tq, LANES), F32), pltpu.VMEM((g, rep * tq, LANES), F32),
                        pltpu.VMEM((g, rep * tq, HEAD_DIM), F32)],
        compiler_params=_params("parallel", "parallel"),
        name="nsa_attend",
    )(qn, kk, proj, kk, proj, sel, expand, o_cmp, gl)


def _nsa(proj, gl, qw3, kw3, pos4, w1, w2, layer, seq, batch, gw, colq, colkv):
    g = NSA_KV_HEADS
    kvw = g * HEAD_DIM
    tables = _rope_tables(jnp.arange(seq))
    qn, kk = _nsa_prep(proj, qw3, kw3, tables, layer, seq, batch, gw, colq, colkv)

    ns = seq // CMP_STRIDE
    n_cmp = (seq - CMP_LEN) // CMP_STRIDE + 1
    cmp_end = jnp.arange(ns) * CMP_STRIDE + CMP_LEN - 1
    cmp_kv = _compress(proj, pos4, w1, w2, kw3, _rope_tables(cmp_end), layer, seq, batch, colkv)

    n_slc = seq // SLC_LEN
    nsp = _round_up(n_slc, LANES)
    ci = np.arange(ns)[:, None] * CMP_STRIDE
    sj = np.arange(nsp)[None, :] * SLC_LEN
    ov = np.maximum(0, np.minimum(ci + CMP_LEN, sj + SLC_LEN) - np.maximum(ci, sj)) / CMP_STRIDE
    ov = np.where((np.arange(ns)[:, None] < n_cmp) & (np.arange(nsp)[None, :] < n_slc), ov, 0.0)
    o_cmp, sel = _cmp_select(qn, cmp_kv, jnp.asarray(ov, BF16), seq, batch, gw)

    blk = 512
    key_blk = (np.arange(seq) // SLC_LEN).reshape(seq // blk, 1, blk)
    expand = jnp.asarray(np.arange(nsp)[None, :, None] == key_blk, BF16)
    return _attend(qn, kk, proj, sel, expand, o_cmp, gl, seq, batch, gw, colkv)


def kernel(x, attn_norm_w, w_in, lru_conv_w, lru_conv_b, lru_w_a, lru_b_a, lru_w_x, lru_b_x, lru_lambda,
           nsa_q_norm_w, nsa_k_norm_w, nsa_cmp_pos, nsa_cmp_w1, nsa_cmp_w2, w_out, mlp_norm_w,
           w_gate, w_up, mlp_conv_w, mlp_conv_b, w_down):
    batch, seq, d = x.shape
    depth = w_in.shape[0]
    gw = d // N_MIXERS
    gh = gw // HEAD_DIM
    kvw = NSA_KV_HEADS * HEAD_DIM
    n_main = 10 * gw + 6 * kvw
    n_gate = w_in.shape[2] - n_main
    t = batch * seq
    assert seq % 512 == 0 and seq >= WINDOW + 128 and gw % (NSA_KV_HEADS * HEAD_DIM) == 0
    assert n_main % 512 == 0 and n_gate <= LANES and d % 1024 == 0 and w_gate.shape[2] % LANES == 0

    w_in_b = w_in.astype(BF16)
    w_out_b = w_out.astype(BF16)
    w_gate_b = w_gate.astype(BF16)
    w_up_b = w_up.astype(BF16)
    w_down_b = w_down.astype(BF16)
    mlp_cb3 = mlp_conv_b[:, None, :]
    attn_w3 = attn_norm_w[:, None, :]
    mlp_w3 = mlp_norm_w[:, None, :]
    lru_cb3, lru_ba3, lru_bx3, lru_lam3 = (v[:, None, :] for v in (lru_conv_b, lru_b_a, lru_b_x, lru_lambda))
    lru_wa_b = lru_w_a.astype(BF16)
    lru_wx_b = lru_w_x.astype(BF16)
    qw3 = nsa_q_norm_w[:, None, :]
    half = CMP_LEN // CMP_STRIDE
    pos4 = nsa_cmp_pos.reshape(depth, 2, half, CMP_STRIDE, HEAD_DIM)
    cmp_w1 = nsa_cmp_w1.reshape(depth, 2, half, CMP_STRIDE, HEAD_DIM, -1).astype(BF16)
    cmp_w2 = nsa_cmp_w2.astype(BF16)

    xf = x.reshape(t, d)
    xb, parts = _row_stats(xf, attn_w3, 0)
    for layer in range(depth):
        proj, r = _inproj(xb, w_in_b, parts, layer, n_main, 1024, 512)
        gl = _gate_logits(xb, w_in_b, r, layer, n_main // LANES, 1024)
        y_ret = _retention(proj, seq, batch, gw)
        y_sb = _stickbreak(proj, seq, batch, gw, 4 * gh)
        y_lru = _rglru(proj, lru_conv_w, lru_cb3, lru_wa_b, lru_ba3, lru_wx_b, lru_bx3, lru_lam3,
                       layer, seq, batch, gw, 7 * gh)
        y_nsa = _nsa(proj, gl, qw3, nsa_k_norm_w, pos4, cmp_w1, cmp_w2, layer, seq, batch, gw,
                     9 * gh, 10 * gh)
        xf, xb, r = _outproj((y_ret, y_sb, y_lru, y_nsa), w_out_b, xf, mlp_w3, layer, 1024, 512)
        hid = _gateup(xb, r, w_gate_b, w_up_b, mlp_conv_w, mlp_cb3, layer, seq, 512, 1024)
        xf, xb, parts = _down(hid, w_down_b, xf, attn_w3, layer, min(layer + 1, depth - 1), 256, 1024)
    return xf.reshape(batch, seq, d)
```

```python
import functools
import math

import numpy as np
import jax
import jax.numpy as jnp
from jax import lax
from jax.experimental import pallas as pl
from jax.experimental.pallas import tpu as pltpu

F32 = jnp.float32
BF16 = jnp.bfloat16

HEAD_DIM = 128
N_MIXERS = 4
RET_CHUNK = 128
RET_ROT_BASE = 10000.0
LRU_CONV = 4
LRU_C = 8.0
NSA_KV_HEADS = 2
CMP_LEN = 32
CMP_STRIDE = 16
SLC_LEN = 64
SLC_TOPK = 16
WINDOW = 512
ROPE_THETA = 500000.0
ROPE_DIMS = HEAD_DIM // 4
MLP_CONV = 3
NORM_EPS = 1e-6

LANES = 128
SUBLANES = 8
VMEM_LIMIT_BYTES = 56 * 1024 * 1024
NEG_BIG = -1e30
FORCED_SCORE = 1e30


def _params(*sem, flags=None):
    return pltpu.CompilerParams(dimension_semantics=sem, vmem_limit_bytes=VMEM_LIMIT_BYTES, flags=flags)


def _round_up(x, m):
    return (x + m - 1) // m * m


def _dot(a, b):
    return jnp.dot(a, b, preferred_element_type=F32)


def _dot_nt(a, b):
    return lax.dot_general(a, b, (((1,), (1,)), ((), ())), preferred_element_type=F32)


def _dot_split(x, m):
    hi = x.astype(BF16)
    lo = (x - hi.astype(F32)).astype(BF16)
    return _dot(hi, m) + _dot(lo, m)


def _sigmoid(x):
    return 1.0 / (1.0 + jnp.exp(-x))


def _softplus(x):
    return jnp.maximum(x, 0.0) + jnp.log1p(jnp.exp(-jnp.abs(x)))


def _gelu_tanh(x):
    return x * (0.5 * (1.0 + jnp.tanh(math.sqrt(2.0 / math.pi) * (x + 0.044715 * (x * x * x)))))


def _rms_rows(x):
    return x * lax.rsqrt(jnp.mean(x * x, axis=-1, keepdims=True) + NORM_EPS)


def _fold_lanes(v):
    out = v[:, 0:LANES]
    for c in range(1, v.shape[1] // LANES):
        out = out + v[:, c * LANES:(c + 1) * LANES]
    return out


def _inv_rms(folded, d):
    total = jnp.sum(folded, axis=-1, keepdims=True)
    return jnp.broadcast_to(lax.rsqrt(total * (1.0 / d) + NORM_EPS), folded.shape)


def _widen(r, n):
    return jnp.concatenate([r] * (n // LANES), axis=1)


def _row_stats_body(x_ref, gain_ref, xb_ref, p_ref):
    x = x_ref[...]
    xb_ref[...] = (x * gain_ref[0]).astype(BF16)
    p_ref[0] = _fold_lanes(x * x)


def _row_stats(x, gain3, layer):
    t, d = x.shape
    tr = 256
    return pl.pallas_call(
        _row_stats_body,
        grid=(t // tr,),
        in_specs=[pl.BlockSpec((tr, d), lambda i: (i, 0)), pl.BlockSpec((1, 1, d), lambda i: (layer, 0, 0))],
        out_specs=[pl.BlockSpec((tr, d), lambda i: (i, 0)), pl.BlockSpec((1, tr, LANES), lambda i: (0, i, 0))],
        out_shape=[jax.ShapeDtypeStruct((t, d), BF16), jax.ShapeDtypeStruct((1, t, LANES), F32)],
        compiler_params=_params("parallel"),
        name="row_stats",
    )(x, gain3)


def _inproj_body(a_ref, w_ref, wg_ref, p_ref, o_ref, gl_ref, scaled, *, d):
    @pl.when(pl.program_id(1) == 0)
    def _():
        folded = p_ref[0]
        for n in range(1, p_ref.shape[0]):
            folded = folded + p_ref[n]
        r = _inv_rms(folded, d)
        scaled[...] = (a_ref[...].astype(F32) * _widen(r, d)).astype(BF16)
        gl_ref[...] = _dot(scaled[...], wg_ref[0])

    o_ref[...] = _dot(scaled[...], w_ref[0]).astype(o_ref.dtype)


def _inproj(a, w, parts, layer, n, gate_col, tm, tn):
    t, k = a.shape
    return pl.pallas_call(
        functools.partial(_inproj_body, d=k),
        grid=(t // tm, n // tn),
        in_specs=[pl.BlockSpec((tm, k), lambda i, j: (i, 0)),
                  pl.BlockSpec((1, k, tn), lambda i, j: (layer, 0, j)),
                  pl.BlockSpec((1, k, LANES), lambda i, j: (layer, 0, gate_col)),
                  pl.BlockSpec((parts.shape[0], tm, LANES), lambda i, j: (0, i, 0))],
        out_specs=[pl.BlockSpec((tm, tn), lambda i, j: (i, j)), pl.BlockSpec((tm, LANES), lambda i, j: (i, 0))],
        out_shape=[jax.ShapeDtypeStruct((t, n), BF16), jax.ShapeDtypeStruct((t, LANES), F32)],
        scratch_shapes=[pltpu.VMEM((tm, k), BF16)],
        compiler_params=_params("parallel", "arbitrary"),
        name="inproj",
    )(a, w, w, parts)


def _outproj_body(y0_ref, y1_ref, y2_ref, y3_ref, w_ref, x_ref, gain_ref, o_ref, ob_ref, r_ref, ssq, *, d):
    j = pl.program_id(1)
    gw = y0_ref.shape[1]
    acc = x_ref[...]
    for n, y_ref in enumerate((y0_ref, y1_ref, y2_ref, y3_ref)):
        acc = acc + _dot(y_ref[...], w_ref[0, n * gw:(n + 1) * gw, :])
    o_ref[...] = acc
    ob_ref[...] = (acc * gain_ref[0]).astype(BF16)
    folded = _fold_lanes(acc * acc)

    @pl.when(j == 0)
    def _():
        ssq[...] = folded

    @pl.when(j > 0)
    def _():
        ssq[...] += folded

    @pl.when(j == pl.num_programs(1) - 1)
    def _():
        r_ref[...] = _inv_rms(ssq[...], d)


def _outproj(ys, w, x, gain3, layer, tm, tn):
    t, d = x.shape
    gw = ys[0].shape[1]
    y_spec = pl.BlockSpec((tm, gw), lambda i, j: (i, 0))
    tile = pl.BlockSpec((tm, tn), lambda i, j: (i, j))
    return pl.pallas_call(
        functools.partial(_outproj_body, d=d),
        grid=(t // tm, d // tn),
        in_specs=[y_spec, y_spec, y_spec, y_spec,
                  pl.BlockSpec((1, N_MIXERS * gw, tn), lambda i, j: (layer, 0, j)), tile,
                  pl.BlockSpec((1, 1, tn), lambda i, j: (layer, 0, j))],
        out_specs=[tile, tile, pl.BlockSpec((tm, LANES), lambda i, j: (i, 0))],
        out_shape=[jax.ShapeDtypeStruct((t, d), F32), jax.ShapeDtypeStruct((t, d), BF16),
                   jax.ShapeDtypeStruct((t, LANES), F32)],
        scratch_shapes=[pltpu.VMEM((tm, LANES), F32)],
        compiler_params=_params("parallel", "arbitrary"),
        name="outproj",
    )(*ys, w, x, gain3)


def _gateup_body(a_ref, r_ref, wg_ref, wu_ref, cw_ref, cb_ref, o_ref, gbuf, *, tiles_per_seq):
    i = pl.program_id(1)
    tm = a_ref.shape[0]
    tn = o_ref.shape[1]

    @pl.when(i % tiles_per_seq == 0)
    def _():
        gbuf[0:SUBLANES, :] = jnp.zeros((SUBLANES, tn), F32)

    r = _widen(r_ref[...], tn)
    g = _dot(a_ref[...], wg_ref[0]) * r
    u = _dot(a_ref[...], wu_ref[0]) * r
    gbuf[SUBLANES:SUBLANES + tm, :] = g
    cw = cw_ref[0]
    gt = (cb_ref[0] + cw[2:3, :] * g
          + cw[1:2, :] * gbuf[SUBLANES - 1:SUBLANES - 1 + tm, :]
          + cw[0:1, :] * gbuf[SUBLANES - 2:SUBLANES - 2 + tm, :])
    o_ref[...] = (gt * _sigmoid(gt) * u).astype(o_ref.dtype)
    gbuf[0:SUBLANES, :] = g[tm - SUBLANES:tm, :]


def _gateup(a, r, wg, wu, cw, cb3, layer, seq, tm, tn):
    t, k = a.shape
    f = wg.shape[2]
    w_spec = pl.BlockSpec((1, k, tn), lambda j, i: (layer, 0, j))
    return pl.pallas_call(
        functools.partial(_gateup_body, tiles_per_seq=seq // tm),
        grid=(pl.cdiv(f, tn), t // tm),
        in_specs=[pl.BlockSpec((tm, k), lambda j, i: (i, 0)),
                  pl.BlockSpec((tm, LANES), lambda j, i: (i, 0)), w_spec, w_spec,
                  pl.BlockSpec((1, MLP_CONV, tn), lambda j, i: (layer, 0, j)),
                  pl.BlockSpec((1, 1, tn), lambda j, i: (layer, 0, j))],
        out_specs=pl.BlockSpec((tm, tn), lambda j, i: (i, j)),
        out_shape=jax.ShapeDtypeStruct((t, f), BF16),
        scratch_shapes=[pltpu.VMEM((SUBLANES + tm, tn), F32)],
        compiler_params=_params("parallel", "arbitrary"),
        name="mlp_gateup",
    )(a, r, wg, wu, cw, cb3)


def _down_body(h_ref, w_ref, x_ref, gain_ref, o_ref, ob_ref, p_ref):
    acc = x_ref[...] + _dot(h_ref[...], w_ref[0])
    o_ref[...] = acc
    ob_ref[...] = (acc * gain_ref[0]).astype(BF16)
    p_ref[0] = _fold_lanes(acc * acc)


def _down(h, w, x, gain3, layer, gain_layer, tm, tn):
    t, f = h.shape
    d = x.shape[1]
    tile = pl.BlockSpec((tm, tn), lambda j, i: (i, j))
    return pl.pallas_call(
        _down_body,
        grid=(d // tn, t // tm),
        in_specs=[pl.BlockSpec((tm, f), lambda j, i: (i, 0)),
                  pl.BlockSpec((1, f, tn), lambda j, i: (layer, 0, j), pipeline_mode=pl.Buffered(1)),
                  tile,
                  pl.BlockSpec((1, 1, tn), lambda j, i: (gain_layer, 0, j))],
        out_specs=[tile, tile, pl.BlockSpec((1, tm, LANES), lambda j, i: (j, i, 0))],
        out_shape=[jax.ShapeDtypeStruct((t, d), F32), jax.ShapeDtypeStruct((t, d), BF16),
                   jax.ShapeDtypeStruct((d // tn, t, LANES), F32)],
        compiler_params=_params("parallel", "parallel"),
        name="mlp_down",
    )(h, w, x, gain3)


def _retention_body(q_ref, k_ref, v_ref, g_ref, cos_ref, sin_ref, inner_ref, qd_ref, kd_ref, cd_ref,
                    o_ref, state, *, heads):
    @pl.when(pl.program_id(1) == 0)
    def _():
        state[...] = jnp.zeros(state.shape, F32)

    cos = cos_ref[...]
    sin = sin_ref[...]

    def rot(x):
        return x * cos + pltpu.roll(x, HEAD_DIM // 2, axis=1) * sin

    for h in range(heads):
        cols = slice(h * HEAD_DIM, (h + 1) * HEAD_DIM)
        q = rot(q_ref[:, cols].astype(F32))
        k = rot(k_ref[:, cols].astype(F32)) * (HEAD_DIM ** -0.5)
        vb = v_ref[:, cols]
        qb = q.astype(BF16)
        inner = _dot_nt(qb, k.astype(BF16)) * inner_ref[h]
        st = state[h]
        o = _dot(inner.astype(BF16), vb) + _dot(qb, st.astype(BF16)) * qd_ref[h]
        kd = (k * kd_ref[h]).T.astype(BF16)
        state[h] = st * cd_ref[h] + _dot(kd, vb)
        g = g_ref[:, cols].astype(F32)
        o_ref[:, cols] = (_rms_rows(o) * (g * _sigmoid(g))).astype(o_ref.dtype)


def _retention(proj, seq, batch, gw):
    t = proj.shape[0]
    heads = gw // HEAD_DIM
    c = RET_CHUNK
    nc = seq // c
    pos = jnp.arange(seq)
    inv = 1.0 / (RET_ROT_BASE ** jnp.linspace(0.0, 1.0, HEAD_DIM // 2, dtype=F32))
    ang = pos.astype(F32)[:, None] * inv[None, :]
    cos, sin = jnp.cos(ang), jnp.sin(ang)
    cos_t = jnp.concatenate([cos, cos], axis=1)
    sin_t = jnp.concatenate([-sin, sin], axis=1)
    log_g = jnp.log(1.0 - 2.0 ** (-5.0 - jnp.arange(heads, dtype=F32)))
    idx = jnp.arange(c, dtype=F32)
    rel = idx[:, None] - idx[None, :]
    inner_decay = jnp.where(rel >= 0, jnp.exp(log_g[:, None, None] * jnp.maximum(rel, 0.0)), 0.0)
    q_decay = jnp.broadcast_to(jnp.exp(log_g[:, None] * (idx + 1.0))[..., None], (heads, c, HEAD_DIM))
    k_decay = jnp.broadcast_to(jnp.exp(log_g[:, None] * (c - 1.0 - idx))[..., None], (heads, c, HEAD_DIM))
    chunk_decay = jnp.broadcast_to(jnp.exp(log_g * c)[:, None, None], (heads, HEAD_DIM, HEAD_DIM))

    def col(n):
        return pl.BlockSpec((c, gw), lambda b, i: (b * nc + i, n))

    rope_spec = pl.BlockSpec((c, HEAD_DIM), lambda b, i: (i, 0))
    const_spec = pl.BlockSpec((heads, c, HEAD_DIM), lambda b, i: (0, 0, 0))
    return pl.pallas_call(
        functools.partial(_retention_body, heads=heads),
        grid=(batch, nc),
        in_specs=[col(0), col(1), col(2), col(3), rope_spec, rope_spec,
                  const_spec, const_spec, const_spec, const_spec],
        out_specs=pl.BlockSpec((c, gw), lambda b, i: (b * nc + i, 0)),
        out_shape=jax.ShapeDtypeStruct((t, gw), BF16),
        scratch_shapes=[pltpu.VMEM((heads, HEAD_DIM, HEAD_DIM), F32)],
        compiler_params=_params("parallel", "arbitrary"),
        name="retention",
    )(proj, proj, proj, proj, cos_t, sin_t, inner_decay, q_decay, k_decay, chunk_decay)


def _stickbreak_body(q_ref, k_ref, v_ref, tri_ref, o_ref, acc_ref, later_ref, *, bk, heads):
    i = pl.program_id(2)
    tq = q_ref.shape[0]
    tri = tri_ref[...]
    log2e = 1.0 / math.log(2.0)
    acc_ref[...] = jnp.zeros(acc_ref.shape, F32)
    later_ref[...] = jnp.zeros(later_ref.shape, F32)
    sign_bit = jnp.uint32(0x80000000)

    def tile(j, row0):
        masked = row0 is not None
        r0 = row0 if masked else 0
        nr = tq - r0
        start = pl.multiple_of(j * bk, bk)
        if masked:
            qpos = i * tq + r0 + lax.broadcasted_iota(jnp.int32, (nr, bk), 0)
            past = start + lax.broadcasted_iota(jnp.int32, (nr, bk), 1) < qpos
        for h in range(heads):
            cols = slice(h * HEAD_DIM, (h + 1) * HEAD_DIM)
            y = _dot_nt(q_ref[r0:, cols], k_ref[pl.ds(start, bk), cols]) * (HEAD_DIM ** -0.5 * log2e)
            minus_abs = pltpu.bitcast(pltpu.bitcast(y, jnp.uint32) | sign_bit, F32)
            stay = jnp.maximum(y, 0.0) + jnp.log(1.0 + jnp.exp2(minus_abs)) * log2e
            if masked:
                stay = jnp.where(past, stay, 0.0)
            hi = stay.astype(BF16)
            both = _dot(hi, tri)
            later = later_ref[h, r0:, :]
            w = jnp.exp2(y - (both + later))
            if masked:
                w = jnp.where(past, w, 0.0)
            acc_ref[h, r0:, :] += _dot(w.astype(BF16), v_ref[pl.ds(start, bk), cols])
            later_ref[h, r0:, :] = later + jnp.sum(stay, axis=1, keepdims=True)

    n_diag = tq // bk
    for d in reversed(range(n_diag)):
        tile(i * n_diag + d, d * bk)

    def body(jj, c):
        tile(i * n_diag - 1 - jj, None)
        return c

    lax.fori_loop(0, i * n_diag, body, 0)
    for h in range(heads):
        o_ref[:, h * HEAD_DIM:(h + 1) * HEAD_DIM] = acc_ref[h].astype(o_ref.dtype)


def _stickbreak(proj, seq, batch, gw, col0):
    t = proj.shape[0]
    tq, bk = 512, 256
    nq = seq // tq
    hb = gw // HEAD_DIM
    wb = hb * HEAD_DIM
    c0 = col0 * HEAD_DIM // wb
    per = gw // wb
    ii = np.arange(bk)
    tri = jnp.asarray(ii[:, None] >= ii[None, :], BF16)
    kv_spec = lambda n: pl.BlockSpec((seq, wb), lambda b, h, i: (b, c0 + n * per + h))
    return pl.pallas_call(
        functools.partial(_stickbreak_body, bk=bk, heads=hb),
        grid=(batch, per, nq),
        in_specs=[pl.BlockSpec((tq, wb), lambda b, h, i: (b * nq + i, c0 + h)),
                  kv_spec(1), kv_spec(2),
                  pl.BlockSpec((bk, bk), lambda b, h, i: (0, 0))],
        out_specs=pl.BlockSpec((tq, wb), lambda b, h, i: (b * nq + i, h)),
        out_shape=jax.ShapeDtypeStruct((t, gw), BF16),
        scratch_shapes=[pltpu.VMEM((hb, tq, HEAD_DIM), F32), pltpu.VMEM((hb, tq, 1), F32)],
        compiler_params=_params("parallel", "parallel", "arbitrary"),
        name="stickbreak",
    )(proj, proj, proj, tri)


def _rglru_body(gate_ref, rec_ref, cw_ref, cb_ref, wa_ref, ba_ref, wx_ref, bx_ref, lam_ref, o_ref,
                xbuf, abuf, ubuf, hprev, *, blocks):
    i = pl.program_id(1)
    ts = rec_ref.shape[0]
    gw = rec_ref.shape[1]

    @pl.when(i == 0)
    def _():
        xbuf[0:SUBLANES, :] = jnp.zeros((SUBLANES, gw), F32)
        hprev[...] = jnp.zeros(hprev.shape, F32)

    x = rec_ref[...].astype(F32)
    xbuf[SUBLANES:SUBLANES + ts, :] = x
    cw = cw_ref[0]
    xr = cb_ref[0] + cw[3:4, :] * x
    for k in range(LRU_CONV - 1):
        off = SUBLANES - (LRU_CONV - 1) + k
        xr = xr + cw[k:k + 1, :] * xbuf[off:off + ts, :]
    xbuf[0:SUBLANES, :] = x[ts - SUBLANES:ts, :]

    log_sig_lam = -_softplus(-lam_ref[0])
    xrb = xr.astype(BF16)
    bw = gw // blocks
    first = (lax.broadcasted_iota(jnp.int32, (ts, bw), 0) == 0) & (i == 0)
    for n in range(blocks):
        cols = slice(n * bw, (n + 1) * bw)
        r = _sigmoid(_dot(xrb[:, cols], wa_ref[0, n]) + ba_ref[0, :, cols])
        gi = _sigmoid(_dot(xrb[:, cols], wx_ref[0, n]) + bx_ref[0, :, cols])
        log_a = LRU_C * r * log_sig_lam[:, cols]
        mult = jnp.where(first, 1.0, jnp.sqrt(jnp.maximum(1.0 - jnp.exp(2.0 * log_a), 0.0)))
        abuf[:, cols] = jnp.exp(log_a)
        ubuf[:, cols] = mult * (gi * xr[:, cols])

    rows = lax.broadcasted_iota(jnp.int32, (SUBLANES, gw), 0)

    def group(gidx, h):
        start = pl.multiple_of(gidx * SUBLANES, SUBLANES)
        a = abuf[pl.ds(start, SUBLANES), :]
        u = ubuf[pl.ds(start, SUBLANES), :]
        for s in (1, 2, 4):
            keep = rows >= s
            a_sh = jnp.where(keep, pltpu.roll(a, s, axis=0), 1.0)
            u_sh = jnp.where(keep, pltpu.roll(u, s, axis=0), 0.0)
            u = u + a * u_sh
            a = a * a_sh
        hs = u + a * h
        ubuf[pl.ds(start, SUBLANES), :] = hs
        return jnp.broadcast_to(hs[SUBLANES - 1:SUBLANES, :], (SUBLANES, gw))

    hprev[...] = lax.fori_loop(0, ts // SUBLANES, group, hprev[...])
    o_ref[...] = (ubuf[...] * _gelu_tanh(gate_ref[...].astype(F32))).astype(o_ref.dtype)


def _rglru(proj, cw, cb3, wa, ba3, wx, bx3, lam3, layer, seq, batch, gw, col0):
    t = proj.shape[0]
    blocks = gw // HEAD_DIM
    ts = 256
    ns = seq // ts
    cpb = col0 // blocks
    vec_spec = pl.BlockSpec((1, 1, gw), lambda b, i: (layer, 0, 0))
    w_spec = pl.BlockSpec((1, blocks, HEAD_DIM, HEAD_DIM), lambda b, i: (layer, 0, 0, 0))
    return pl.pallas_call(
        functools.partial(_rglru_body, blocks=blocks),
        grid=(batch, ns),
        in_specs=[pl.BlockSpec((ts, gw), lambda b, i: (b * ns + i, cpb)),
                  pl.BlockSpec((ts, gw), lambda b, i: (b * ns + i, cpb + 1)),
                  pl.BlockSpec((1, LRU_CONV, gw), lambda b, i: (layer, 0, 0)),
                  vec_spec, w_spec, vec_spec, w_spec, vec_spec, vec_spec],
        out_specs=pl.BlockSpec((ts, gw), lambda b, i: (b * ns + i, 0)),
        out_shape=jax.ShapeDtypeStruct((t, gw), BF16),
        scratch_shapes=[pltpu.VMEM((SUBLANES + ts, gw), F32), pltpu.VMEM((ts, gw), F32),
                        pltpu.VMEM((ts, gw), F32), pltpu.VMEM((SUBLANES, gw), F32)],
        compiler_params=_params("parallel", "arbitrary"),
        name="rglru",
    )(proj, proj, cw, cb3, wa, ba3, wx, bx3, lam3)


def _rope_tables(pos):
    half = ROPE_DIMS // 2
    inv = ROPE_THETA ** (-jnp.arange(half, dtype=F32) / half)
    ang = pos.astype(F32)[:, None] * inv[None, :]
    cos, sin = jnp.cos(ang), jnp.sin(ang)
    n = pos.shape[0]
    zeros = jnp.zeros((n, half), F32)
    rest = HEAD_DIM - 2 * half
    c = jnp.concatenate([cos, cos, jnp.ones((n, rest), F32)], axis=1)
    s_lo = jnp.concatenate([-sin, zeros, jnp.zeros((n, rest), F32)], axis=1)
    s_hi = jnp.concatenate([zeros, sin, jnp.zeros((n, rest), F32)], axis=1)
    return c, s_lo, s_hi


def _partial_rope(x, c, s_lo, s_hi):
    half = ROPE_DIMS // 2
    return x * c + pltpu.roll(x, HEAD_DIM - half, axis=1) * s_lo + pltpu.roll(x, half, axis=1) * s_hi


def _nsa_prep_body(q_ref, ks_ref, kw_ref, qw_ref, kwt_ref, c_ref, lo_ref, hi_ref, qo_ref, ko_ref,
                   *, heads, kv_heads):
    c, s_lo, s_hi = c_ref[...], lo_ref[...], hi_ref[...]

    def prep(x, w):
        return _partial_rope(_rms_rows(x) * w, c, s_lo, s_hi)

    qw = qw_ref[0]
    for h in range(heads):
        cols = slice(h * HEAD_DIM, (h + 1) * HEAD_DIM)
        qo_ref[:, cols] = (prep(q_ref[:, cols].astype(F32), qw) * (HEAD_DIM ** -0.5)).astype(qo_ref.dtype)
    for n, src in enumerate((ks_ref, kw_ref)):
        w = kwt_ref[0, n + 1:n + 2, :]
        for g in range(kv_heads):
            cols = slice(g * HEAD_DIM, (g + 1) * HEAD_DIM)
            dst = slice((n * kv_heads + g) * HEAD_DIM, (n * kv_heads + g + 1) * HEAD_DIM)
            ko_ref[:, dst] = prep(src[:, cols].astype(F32), w).astype(ko_ref.dtype)


def _nsa_prep(proj, qw3, kw3, tables, layer, seq, batch, gw, colq, colkv):
    t = proj.shape[0]
    heads = gw // HEAD_DIM
    kvw = NSA_KV_HEADS * HEAD_DIM
    ts = 256
    ns = seq // ts
    rope_spec = pl.BlockSpec((ts, HEAD_DIM), lambda b, i: (i, 0))
    cq = colq * HEAD_DIM // gw
    ck = colkv * HEAD_DIM // kvw
    return pl.pallas_call(
        functools.partial(_nsa_prep_body, heads=heads, kv_heads=NSA_KV_HEADS),
        grid=(batch, ns),
        in_specs=[pl.BlockSpec((ts, gw), lambda b, i: (b * ns + i, cq)),
                  pl.BlockSpec((ts, kvw), lambda b, i: (b * ns + i, ck + 2)),
                  pl.BlockSpec((ts, kvw), lambda b, i: (b * ns + i, ck + 4)),
                  pl.BlockSpec((1, 1, HEAD_DIM), lambda b, i: (layer, 0, 0)),
                  pl.BlockSpec((1, 3, HEAD_DIM), lambda b, i: (layer, 0, 0)),
                  rope_spec, rope_spec, rope_spec],
        out_specs=[pl.BlockSpec((ts, gw), lambda b, i: (b * ns + i, 0)),
                   pl.BlockSpec((ts, 2 * kvw), lambda b, i: (b * ns + i, 0))],
        out_shape=[jax.ShapeDtypeStruct((t, gw), BF16), jax.ShapeDtypeStruct((t, 2 * kvw), BF16)],
        compiler_params=_params("parallel", "parallel"),
        name="nsa_prep",
    )(proj, proj, proj, qw3, kw3, *tables)


def _compress_body(x_ref, pos_ref, w1_ref, w2_ref, kw_ref, c_ref, lo_ref, hi_ref, o_ref, xs):
    xs[...] = x_ref[...].astype(F32)
    ns = o_ref.shape[2]
    hidden = w2_ref.shape[2]
    first = jnp.zeros((ns, hidden), F32)
    second = jnp.zeros((ns, hidden), F32)
    for l in range(CMP_STRIDE):
        rows = xs[pl.ds(l, ns, stride=CMP_STRIDE), :]
        first = first + _dot((rows + pos_ref[0, 0, 0, l:l + 1, :]).astype(BF16), w1_ref[0, 0, 0, l])
        second = second + _dot((rows + pos_ref[0, 0, 1, l:l + 1, :]).astype(BF16), w1_ref[0, 0, 1, l])
    hid = _gelu_tanh(first + pltpu.roll(second, ns - 1, axis=0))
    out = _dot(hid.astype(BF16), w2_ref[0, 0])

    @pl.when(pl.program_id(1) < NSA_KV_HEADS)
    def _():
        o_ref[0, 0] = _partial_rope(_rms_rows(out) * kw_ref[0, 0:1, :], c_ref[...], lo_ref[...], hi_ref[...])

    @pl.when(pl.program_id(1) >= NSA_KV_HEADS)
    def _():
        o_ref[0, 0] = out


def _compress(proj, pos5, w1, w2, kw3, tables, layer, seq, batch, colkv):
    g = NSA_KV_HEADS
    n4 = 2 * g
    ns = seq // CMP_STRIDE
    half = CMP_LEN // CMP_STRIDE
    hidden = w1.shape[-1]
    full = pl.BlockSpec((ns, HEAD_DIM), lambda b, c: (0, 0))
    return pl.pallas_call(
        _compress_body,
        grid=(batch, n4),
        in_specs=[pl.BlockSpec((seq, HEAD_DIM), lambda b, c: (b, colkv + c)),
                  pl.BlockSpec((1, 1, half, CMP_STRIDE, HEAD_DIM), lambda b, c: (layer, c // g, 0, 0, 0)),
                  pl.BlockSpec((1, 1, half, CMP_STRIDE, HEAD_DIM, hidden),
                               lambda b, c: (layer, c // g, 0, 0, 0, 0)),
                  pl.BlockSpec((1, 1, hidden, HEAD_DIM), lambda b, c: (layer, c // g, 0, 0)),
                  pl.BlockSpec((1, 3, HEAD_DIM), lambda b, c: (layer, 0, 0)),
                  full, full, full],
        out_specs=pl.BlockSpec((1, 1, ns, HEAD_DIM), lambda b, c: (b, c, 0, 0)),
        out_shape=jax.ShapeDtypeStruct((batch, n4, ns, HEAD_DIM), F32),
        scratch_shapes=[pltpu.VMEM((seq, HEAD_DIM), F32)],
        compiler_params=_params("parallel", "parallel"),
        name="nsa_compress",
    )(proj, pos5, w1, w2, kw3, *tables)


def _cmp_select_body(q_ref, kc_ref, vc_ref, ov_ref, o_ref, sel_ref, score_t, *, rep, n_cmp, n_slc, top_k):
    i = pl.program_id(2)
    tq = q_ref.shape[0]
    ncp = kc_ref.shape[2]
    nsp = ov_ref.shape[1]
    kc = kc_ref[0, 0].astype(BF16)
    vc = vc_ref[0, 0].astype(BF16)
    tpos = i * tq + lax.broadcasted_iota(jnp.int32, (tq, ncp), 0)
    blk_n = lax.broadcasted_iota(jnp.int32, (tq, ncp), 1)
    visible = (blk_n * CMP_STRIDE + (CMP_LEN - 1) <= tpos) & (blk_n < n_cmp)
    p_sum = jnp.zeros((tq, ncp), F32)
    for r in range(rep):
        cols = slice(r * HEAD_DIM, (r + 1) * HEAD_DIM)
        s = jnp.where(visible, _dot_nt(q_ref[:, cols], kc), -jnp.inf)
        m = jnp.max(s, axis=-1, keepdims=True)
        m = jnp.where(m > -jnp.inf, m, 0.0)
        e = jnp.where(visible, jnp.exp(s - m), 0.0)
        p = e / jnp.maximum(jnp.sum(e, axis=-1, keepdims=True), 1e-30)
        o_ref[:, cols] = _dot(p.astype(BF16), vc)
        p_sum = p_sum + p
    p_slc = _dot_split(p_sum, ov_ref[...])

    tpos_s = i * tq + lax.broadcasted_iota(jnp.int32, (tq, nsp), 0)
    blk_s = lax.broadcasted_iota(jnp.int32, (tq, nsp), 1)
    cur = tpos_s // SLC_LEN
    forced = (blk_s == 0) | (blk_s == cur) | (blk_s == cur - 1)
    valid = blk_s <= cur
    score = jnp.where(forced, FORCED_SCORE, jnp.where(valid, p_slc, NEG_BIG))
    score_t[...] = score.T
    n_groups = _round_up(n_slc, SUBLANES) // SUBLANES
    mine = [score_t[v * SUBLANES:(v + 1) * SUBLANES, :] for v in range(n_groups)]
    rank = [jnp.zeros((SUBLANES, tq), F32) for _ in range(n_groups)]
    sub = lax.broadcasted_iota(jnp.int32, (SUBLANES, tq), 0)
    for other in range(n_slc):
        row = score_t[other:other + 1, :]
        for v in range(n_groups):
            if v < other // SUBLANES:
                ahead = row > mine[v]
            elif v > other // SUBLANES:
                ahead = row >= mine[v]
            else:
                ahead = (row > mine[v]) | ((row == mine[v]) & (sub > other % SUBLANES))
            rank[v] = rank[v] + jnp.where(ahead, 1.0, 0.0)
    for v in range(n_groups):
        score_t[v * SUBLANES:(v + 1) * SUBLANES, :] = jnp.where(rank[v] < top_k, 1.0, 0.0)
    chosen = score_t[...].T
    sel_ref[0] = jnp.where(valid, chosen, 0.0).astype(sel_ref.dtype)


def _cmp_select(qn, cmp_kv, overlap, seq, batch, gw):
    t = qn.shape[0]
    g = NSA_KV_HEADS
    rep = gw // HEAD_DIM // g
    ncp = cmp_kv.shape[2]
    nsp = overlap.shape[1]
    n_slc = seq // SLC_LEN
    tq = 256
    nq = seq // tq
    body = functools.partial(_cmp_select_body, rep=rep, n_cmp=(seq - CMP_LEN) // CMP_STRIDE + 1,
                             n_slc=n_slc, top_k=min(SLC_TOPK, n_slc))
    return pl.pallas_call(
        body,
        grid=(batch, g, nq),
        in_specs=[pl.BlockSpec((tq, rep * HEAD_DIM), lambda b, gi, i: (b * nq + i, gi)),
                  pl.BlockSpec((1, 1, ncp, HEAD_DIM), lambda b, gi, i: (b, gi, 0, 0)),
                  pl.BlockSpec((1, 1, ncp, HEAD_DIM), lambda b, gi, i: (b, g + gi, 0, 0)),
                  pl.BlockSpec((ncp, nsp), lambda b, gi, i: (0, 0))],
        out_specs=[pl.BlockSpec((tq, rep * HEAD_DIM), lambda b, gi, i: (b * nq + i, gi)),
                   pl.BlockSpec((1, tq, nsp), lambda b, gi, i: (gi, b * nq + i, 0))],
        out_shape=[jax.ShapeDtypeStruct((t, gw), F32), jax.ShapeDtypeStruct((g, t, nsp), BF16)],
        scratch_shapes=[pltpu.VMEM((nsp, tq), F32)],
        compiler_params=_params("parallel", "parallel", "parallel"),
        name="nsa_cmp_select",
    )(qn, cmp_kv, cmp_kv, overlap)


def _attend_body(q_ref, ks_ref, vs_ref, kw_ref, vw_ref, sel_ref, ex_ref, oc_ref, gl_ref, o_ref,
                 s_ref, m_ref, l_ref, acc_ref, *, rep, groups, blk, span):
    i = pl.program_id(1)
    tq = q_ref.shape[0]
    rows = rep * tq
    folds = blk // LANES
    n_blocks = ((i + 1) * tq + blk - 1) // blk

    def stacked_q(g):
        return jnp.concatenate(
            [q_ref[:, (g * rep + r) * HEAD_DIM:(g * rep + r + 1) * HEAD_DIM] for r in range(rep)], axis=0)

    m_ref[...] = jnp.full(m_ref.shape, NEG_BIG, F32)
    l_ref[...] = jnp.zeros(l_ref.shape, F32)
    acc_ref[...] = jnp.zeros(acc_ref.shape, F32)
    tpos = i * tq + lax.broadcasted_iota(jnp.int32, (tq, blk), 0)
    kcol = lax.broadcasted_iota(jnp.int32, (tq, blk), 1)

    def scores(j, carry):
        start = pl.multiple_of(j * blk, blk)
        causal = start + kcol <= tpos
        for g in range(groups):
            cols = slice(g * HEAD_DIM, (g + 1) * HEAD_DIM)
            picked = _dot(sel_ref[g], ex_ref[j])
            bias = jnp.where((picked > 0.5) & causal, 0.0, NEG_BIG)
            s = _dot_nt(stacked_q(g), ks_ref[pl.ds(start, blk), cols]) + jnp.concatenate([bias] * rep, axis=0)
            s_ref[g, j] = s
            m = m_ref[g]
            for c in range(folds):
                m = jnp.maximum(m, s[:, c * LANES:(c + 1) * LANES])
            m_ref[g] = m
        return carry

    lax.fori_loop(0, n_blocks, scores, 0)

    for g in range(groups):
        m_ref[g] = jnp.broadcast_to(jnp.max(m_ref[g], axis=-1, keepdims=True), (rows, LANES))

    def weights(j, carry):
        start = pl.multiple_of(j * blk, blk)
        for g in range(groups):
            cols = slice(g * HEAD_DIM, (g + 1) * HEAD_DIM)
            p = jnp.exp(s_ref[g, j] - jnp.concatenate([m_ref[g]] * folds, axis=1))
            l = l_ref[g]
            for c in range(folds):
                l = l + p[:, c * LANES:(c + 1) * LANES]
            l_ref[g] = l
            acc_ref[g] += _dot(p.astype(BF16), vs_ref[pl.ds(start, blk), cols])
        return carry

    lax.fori_loop(0, n_blocks, weights, 0)

    wstart = pl.multiple_of(jnp.maximum(i * tq + tq - span, 0), tq)
    wpos = i * tq + (lax.broadcasted_iota(jnp.int32, (rep * tq, span), 0) & (tq - 1))
    dist = wpos - (wstart + lax.broadcasted_iota(jnp.int32, (rep * tq, span), 1))
    wbias = jnp.where((dist >= 0) & (dist < WINDOW), 0.0, NEG_BIG)
    gates = _sigmoid(gl_ref[...])

    def gate(head, branch):
        c = 3 * head + branch
        return gates[:, c:c + 1]

    for g in range(groups):
        cols = slice(g * HEAD_DIM, (g + 1) * HEAD_DIM)
        s = _dot_nt(stacked_q(g), kw_ref[pl.ds(wstart, span), cols]) + wbias
        e = jnp.exp(s - jnp.max(s, axis=-1, keepdims=True))
        o_win = _dot(e.astype(BF16), vw_ref[pl.ds(wstart, span), cols]) / jnp.sum(e, axis=-1, keepdims=True)
        o_slc = acc_ref[g] / jnp.sum(l_ref[g], axis=-1, keepdims=True)
        for r in range(rep):
            head = g * rep + r
            hc = slice(head * HEAD_DIM, (head + 1) * HEAD_DIM)
            rows = slice(r * tq, (r + 1) * tq)
            mix = gate(head, 0) * oc_ref[:, hc] + gate(head, 1) * o_slc[rows, :] + gate(head, 2) * o_win[rows, :]
            o_ref[:, hc] = mix.astype(o_ref.dtype)


def _attend(qn, kk, proj, sel, expand, o_cmp, gl, seq, batch, gw, colkv):
    t = qn.shape[0]
    g = NSA_KV_HEADS
    kvw = g * HEAD_DIM
    rep = gw // HEAD_DIM // g
    nsp = sel.shape[2]
    tq = 128
    blk = expand.shape[2]
    nq = seq // tq
    span = WINDOW + tq
    ck = colkv * HEAD_DIM // kvw
    row_spec = pl.BlockSpec((tq, gw), lambda b, i: (b * nq + i, 0))
    return pl.pallas_call(
        functools.partial(_attend_body, rep=rep, groups=g, blk=blk, span=span),
        grid=(batch, nq),
        in_specs=[row_spec,
                  pl.BlockSpec((seq, kvw), lambda b, i: (b, 0)),
                  pl.BlockSpec((seq, kvw), lambda b, i: (b, ck + 3)),
                  pl.BlockSpec((seq, kvw), lambda b, i: (b, 1)),
                  pl.BlockSpec((seq, kvw), lambda b, i: (b, ck + 5)),
                  pl.BlockSpec((g, tq, nsp), lambda b, i: (0, b * nq + i, 0)),
                  pl.BlockSpec((seq // blk, nsp, blk), lambda b, i: (0, 0, 0)),
                  row_spec,
                  pl.BlockSpec((tq, LANES), lambda b, i: (b * nq + i, 0))],
        out_specs=row_spec,
        out_shape=jax.ShapeDtypeStruct((t, gw), BF16),
        scratch_shapes=[pltpu.VMEM((g, seq // blk, rep * tq, blk), F32),
                        pltpu.VMEM((g, rep * tq, LANES), F32), pltpu.VMEM((g, rep * tq, LANES), F32),
                        pltpu.VMEM((g, rep * tq, HEAD_DIM), F32)],
        compiler_params=_params("parallel", "parallel"),
        name="nsa_attend",
    )(qn, kk, proj, kk, proj, sel, expand, o_cmp, gl)


def _nsa(proj, gl, qw3, kw3, pos4, w1, w2, layer, seq, batch, gw, colq, colkv):
    g = NSA_KV_HEADS
    kvw = g * HEAD_DIM
    tables = _rope_tables(jnp.arange(seq))
    qn, kk = _nsa_prep(proj, qw3, kw3, tables, layer, seq, batch, gw, colq, colkv)

    ns = seq // CMP_STRIDE
    n_cmp = (seq - CMP_LEN) // CMP_STRIDE + 1
    cmp_end = jnp.arange(ns) * CMP_STRIDE + CMP_LEN - 1
    cmp_kv = _compress(proj, pos4, w1, w2, kw3, _rope_tables(cmp_end), layer, seq, batch, colkv)

    n_slc = seq // SLC_LEN
    nsp = _round_up(n_slc, LANES)
    ci = np.arange(ns)[:, None] * CMP_STRIDE
    sj = np.arange(nsp)[None, :] * SLC_LEN
    ov = np.maximum(0, np.minimum(ci + CMP_LEN, sj + SLC_LEN) - np.maximum(ci, sj)) / CMP_STRIDE
    ov = np.where((np.arange(ns)[:, None] < n_cmp) & (np.arange(nsp)[None, :] < n_slc), ov, 0.0)
    o_cmp, sel = _cmp_select(qn, cmp_kv, jnp.asarray(ov, BF16), seq, batch, gw)

    blk = 512
    key_blk = (np.arange(seq) // SLC_LEN).reshape(seq // blk, 1, blk)
    expand = jnp.asarray(np.arange(nsp)[None, :, None] == key_blk, BF16)
    return _attend(qn, kk, proj, sel, expand, o_cmp, gl, seq, batch, gw, colkv)


def kernel(x, attn_norm_w, w_in, lru_conv_w, lru_conv_b, lru_w_a, lru_b_a, lru_w_x, lru_b_x, lru_lambda,
           nsa_q_norm_w, nsa_k_norm_w, nsa_cmp_pos, nsa_cmp_w1, nsa_cmp_w2, w_out, mlp_norm_w,
           w_gate, w_up, mlp_conv_w, mlp_conv_b, w_down):
    batch, seq, d = x.shape
    depth = w_in.shape[0]
    gw = d // N_MIXERS
    gh = gw // HEAD_DIM
    kvw = NSA_KV_HEADS * HEAD_DIM
    n_main = 10 * gw + 6 * kvw
    n_gate = w_in.shape[2] - n_main
    t = batch * seq
    assert seq % 512 == 0 and seq >= WINDOW + 128 and gw % (NSA_KV_HEADS * HEAD_DIM) == 0
    assert n_main % 512 == 0 and n_gate <= LANES and d % 1024 == 0 and w_gate.shape[2] % LANES == 0

    w_in_b = w_in.astype(BF16)
    w_out_b = w_out.astype(BF16)
    w_gate_b = w_gate.astype(BF16)
    w_up_b = w_up.astype(BF16)
    w_down_b = w_down.astype(BF16)
    mlp_cb3 = mlp_conv_b[:, None, :]
    attn_w3 = attn_norm_w[:, None, :]
    mlp_w3 = mlp_norm_w[:, None, :]
    lru_cb3, lru_ba3, lru_bx3, lru_lam3 = (v[:, None, :] for v in (lru_conv_b, lru_b_a, lru_b_x, lru_lambda))
    lru_wa_b = lru_w_a.astype(BF16)
    lru_wx_b = lru_w_x.astype(BF16)
    qw3 = nsa_q_norm_w[:, None, :]
    half = CMP_LEN // CMP_STRIDE
    pos4 = nsa_cmp_pos.reshape(depth, 2, half, CMP_STRIDE, HEAD_DIM)
    cmp_w1 = nsa_cmp_w1.reshape(depth, 2, half, CMP_STRIDE, HEAD_DIM, -1).astype(BF16)
    cmp_w2 = nsa_cmp_w2.astype(BF16)

    xf = x.reshape(t, d)
    xb, parts = _row_stats(xf, attn_w3, 0)
    for layer in range(depth):
        proj, gl = _inproj(xb, w_in_b, parts, layer, n_main, n_main // LANES, 1024, 512)
        y_ret = _retention(proj, seq, batch, gw)
        y_sb = _stickbreak(proj, seq, batch, gw, 4 * gh)
        y_lru = _rglru(proj, lru_conv_w, lru_cb3, lru_wa_b, lru_ba3, lru_wx_b, lru_bx3, lru_lam3,
                       layer, seq, batch, gw, 7 * gh)
        y_nsa = _nsa(proj, gl, qw3, nsa_k_norm_w, pos4, cmp_w1, cmp_w2, layer, seq, batch, gw,
                     9 * gh, 10 * gh)
        xf, xb, r = _outproj((y_ret, y_sb, y_lru, y_nsa), w_out_b, xf, mlp_w3, layer, 1024, 512)
        hid = _gateup(xb, r, w_gate_b, w_up_b, mlp_conv_w, mlp_cb3, layer, seq, 512, 1024)
        xf, xb, parts = _down(hid, w_down_b, xf, attn_w3, layer, min(layer + 1, depth - 1), 256, 1024)
    return xf.reshape(batch, seq, d)
```
